```python
import jax
import jax.numpy as jnp
from jax import lax
import numpy as np

D_MODEL = 1024
BATCH = 32
SEQ = 2048
DEPTH = 1
DEC_BATCH = 2
DEC_SEQ = 16384
PAST_LEN = 128

D_MIX = D_MODEL
GLA_HEADS = 4
GLA_V = D_MIX // 2
GLA_DV = GLA_V // GLA_HEADS
GLA_DK = GLA_DV // 2
GLA_QK = GLA_HEADS * GLA_DK
GLA_LOWRANK = 16
GLA_NORMALIZER = 16.0
LOG_GATE_MIN = -1.0
RET_HEADS = 4
RET_V = D_MIX - GLA_V
RET_DV = RET_V // RET_HEADS
RET_DK = RET_DV
RET_QK = RET_HEADS * RET_DK
CHUNK = 64
D_FF = 4 * D_MODEL
ROPE_BASE = 10000.0
EPS = 1e-6
IN_WIDTHS = (GLA_QK, GLA_QK, GLA_V, GLA_V, GLA_LOWRANK, GLA_LOWRANK, RET_QK, RET_QK, RET_V, RET_V)
D_IN = 2 * GLA_QK + 2 * GLA_V + 2 * GLA_LOWRANK + 2 * RET_QK + 2 * RET_V

kernel_name = "hybrid_gla_retention_encoder"


def rmsnorm(x, w):
    x32 = x.astype(jnp.float32)
    y = x32 * lax.rsqrt(jnp.mean(x32 * x32, axis=-1, keepdims=True) + EPS)
    return (y * w.astype(jnp.float32)).astype(x.dtype)


def head_rmsnorm(o, w):
    return o * lax.rsqrt(jnp.mean(o * o, axis=-1, keepdims=True) + EPS) * w.astype(jnp.float32)


def head_groupnorm(o, w, b):
    H, dv = o.shape[-2], o.shape[-1]
    mu = jnp.mean(o, axis=-1, keepdims=True)
    var = jnp.mean(jnp.square(o - mu), axis=-1, keepdims=True)
    y = (o - mu) * lax.rsqrt(var + EPS)
    return y * w.astype(jnp.float32).reshape(H, dv) + b.astype(jnp.float32).reshape(H, dv)


def rotary(x, pos):
    half = x.shape[-1] // 2
    inv_freq = jnp.power(ROPE_BASE, -jnp.arange(half, dtype=jnp.float32) / half)
    ang = pos[:, None] * inv_freq[None, :]
    cos = jnp.cos(ang)[None, :, None, :]
    sin = jnp.sin(ang)[None, :, None, :]
    x1, x2 = x[..., :half], x[..., half:]
    return jnp.concatenate([x1 * cos - x2 * sin, x1 * sin + x2 * cos], axis=-1)


def split_columns(proj):
    outs = []
    start = 0
    for width in IN_WIDTHS:
        outs.append(proj[..., start:start + width])
        start += width
    return outs


def chunked_gated_recurrence(q, k, v, log_a, inclusive):
    B, L, H, dk = q.shape
    dv = v.shape[-1]
    n = L // CHUNK

    def blk(t):
        return t.astype(jnp.float32).reshape(B, n, CHUNK, H, t.shape[-1])

    q, k, v, log_a = blk(q), blk(k), blk(v), blk(log_a)
    b = jnp.cumsum(log_a, axis=2)
    b_last = b[:, :, -1:]
    q_dec = q * jnp.exp(b)
    k_inv = k * jnp.exp(-b)
    k_end = k * jnp.exp(b_last - b)
    scores = jnp.einsum('bncht,bnsht->bnhcs', q_dec, k_inv)
    mask = jnp.tril(jnp.ones((CHUNK, CHUNK), dtype=bool), k=0 if inclusive else -1)
    scores = jnp.where(mask, scores, 0.0)
    o_intra = jnp.einsum('bnhcs,bnshv->bnchv', scores, v)
    chunk_kv = jnp.einsum('bncht,bnchv->bnhtv', k_end, v)
    chunk_decay = jnp.exp(b_last[:, :, 0])

    def step(state, inp):
        kv, dec = inp
        return dec[..., None] * state + kv, state

    init = jnp.zeros((B, H, dk, dv), jnp.float32)
    _, states = lax.scan(step, init, (jnp.moveaxis(chunk_kv, 1, 0), jnp.moveaxis(chunk_decay, 1, 0)))
    states = jnp.moveaxis(states, 0, 1)
    o_inter = jnp.einsum('bncht,bnhtv->bnchv', q_dec, states)
    return (o_intra + o_inter).reshape(B, L, H, dv)


def bidirectional_recurrence(q, k, v, log_a_fwd, log_a_bwd):
    flip = lambda t: jnp.flip(t, axis=1)
    fwd = chunked_gated_recurrence(q, k, v, log_a_fwd, True)
    bwd = flip(chunked_gated_recurrence(flip(q), flip(k), flip(v), flip(log_a_bwd), False))
    return fwd + bwd


def encoder_layer(x, attn_norm_w, w_in, w_alpha_fwd, b_alpha_fwd, w_alpha_bwd, b_alpha_bwd,
                  gla_norm_w, ret_norm_w, ret_norm_b, w_out, mlp_norm_w, w_ff1, w_ff2):
    B, L, _ = x.shape
    f32 = jnp.float32
    h = rmsnorm(x, attn_norm_w)
    proj = jnp.einsum('bld,de->ble', h, w_in)
    gq, gk, gv, gg, ga_f, ga_b, rq, rk, rv, rg = split_columns(proj)

    def heads(t, n_heads):
        return t.astype(f32).reshape(B, L, n_heads, -1)

    gla_q = heads(gq, GLA_HEADS) * GLA_DK ** -0.5
    gla_k = heads(gk, GLA_HEADS)
    gla_v = heads(gv, GLA_HEADS)

    def log_gate(lr, w, b):
        z = jnp.einsum('blr,rk->blk', lr.astype(f32), w.astype(f32)) + b.astype(f32)
        log_a = jnp.maximum(jax.nn.log_sigmoid(z) / GLA_NORMALIZER, LOG_GATE_MIN)
        return log_a.reshape(B, L, GLA_HEADS, GLA_DK)

    log_a_f = log_gate(ga_f, w_alpha_fwd, b_alpha_fwd)
    log_a_b = log_gate(ga_b, w_alpha_bwd, b_alpha_bwd)
    o_gla = bidirectional_recurrence(gla_q, gla_k, gla_v, log_a_f, log_a_b)
    o_gla = head_rmsnorm(o_gla, gla_norm_w).reshape(B, L, GLA_V) * jax.nn.silu(gg.astype(f32))

    pos = jnp.arange(L, dtype=f32)
    ret_q = rotary(heads(rq, RET_HEADS), pos) * RET_DK ** -0.5
    ret_k = rotary(heads(rk, RET_HEADS), pos)
    ret_v = heads(rv, RET_HEADS)
    log_gamma = jnp.log1p(-jnp.power(2.0, -5.0 - jnp.arange(RET_HEADS, dtype=f32)))
    log_decay = jnp.broadcast_to(log_gamma[None, None, :, None], (B, L, RET_HEADS, RET_DK))
    o_ret = bidirectional_recurrence(ret_q, ret_k, ret_v, log_decay, log_decay)
    o_ret = head_groupnorm(o_ret, ret_norm_w, ret_norm_b).reshape(B, L, RET_V) * jax.nn.silu(rg.astype(f32))

    mixed = jnp.concatenate([o_gla, o_ret], axis=-1).astype(x.dtype)
    x = x + jnp.einsum('ble,ed->bld', mixed, w_out)

    h = rmsnorm(x, mlp_norm_w)
    ff = jnp.square(jax.nn.relu(jnp.einsum('bld,df->blf', h, w_ff1)))
    return x + jnp.einsum('blf,fd->bld', ff, w_ff2)


def trunk(x, attn_norm_w, w_in, w_alpha_fwd, b_alpha_fwd, w_alpha_bwd, b_alpha_bwd,
          gla_norm_w, ret_norm_w, ret_norm_b, w_out, mlp_norm_w, w_ff1, w_ff2, final_norm_w):
    for l in range(DEPTH):
        x = encoder_layer(x, attn_norm_w[l], w_in[l], w_alpha_fwd[l], b_alpha_fwd[l],
                          w_alpha_bwd[l], b_alpha_bwd[l], gla_norm_w[l], ret_norm_w[l],
                          ret_norm_b[l], w_out[l], mlp_norm_w[l], w_ff1[l], w_ff2[l])
    return rmsnorm(x, final_norm_w)


def setup_inputs(seed: int = 0) -> dict:
    key = jax.random.key(seed)
    ks = jax.random.split(key, 16)
    f32 = jnp.float32
    nrm = lambda k, shape: jax.random.normal(k, shape, f32)
    return {
        'x_prompt': nrm(ks[0], (BATCH, SEQ, D_MODEL)),
        'x_sample': nrm(ks[1], (DEC_BATCH, DEC_SEQ, D_MODEL)),
        'attn_norm_w': 1.0 + 0.02 * nrm(ks[2], (DEPTH, D_MODEL)),
        'w_in': nrm(ks[3], (DEPTH, D_MODEL, D_IN)) * D_MODEL ** -0.5,
        'w_alpha_fwd': nrm(ks[4], (DEPTH, GLA_LOWRANK, GLA_QK)) * GLA_LOWRANK ** -0.5,
        'b_alpha_fwd': 0.1 * nrm(ks[5], (DEPTH, GLA_QK)),
        'w_alpha_bwd': nrm(ks[6], (DEPTH, GLA_LOWRANK, GLA_QK)) * GLA_LOWRANK ** -0.5,
        'b_alpha_bwd': 0.1 * nrm(ks[7], (DEPTH, GLA_QK)),
        'gla_norm_w': 1.0 + 0.02 * nrm(ks[8], (DEPTH, GLA_DV)),
        'ret_norm_w': 1.0 + 0.02 * nrm(ks[9], (DEPTH, RET_V)),
        'ret_norm_b': 0.02 * nrm(ks[10], (DEPTH, RET_V)),
        'w_out': nrm(ks[11], (DEPTH, D_MIX, D_MODEL)) * D_MIX ** -0.5,
        'mlp_norm_w': 1.0 + 0.02 * nrm(ks[12], (DEPTH, D_MODEL)),
        'w_ff1': nrm(ks[13], (DEPTH, D_MODEL, D_FF)) * D_MODEL ** -0.5,
        'w_ff2': nrm(ks[14], (DEPTH, D_FF, D_MODEL)) * D_FF ** -0.5,
        'final_norm_w': 1.0 + 0.02 * nrm(ks[15], (D_MODEL,)),
    }


def reference(x_prompt, x_sample, attn_norm_w, w_in, w_alpha_fwd, b_alpha_fwd, w_alpha_bwd, b_alpha_bwd,
              gla_norm_w, ret_norm_w, ret_norm_b, w_out, mlp_norm_w, w_ff1, w_ff2, final_norm_w):
    y_prompt = trunk(x_prompt, attn_norm_w, w_in, w_alpha_fwd, b_alpha_fwd, w_alpha_bwd, b_alpha_bwd,
                     gla_norm_w, ret_norm_w, ret_norm_b, w_out, mlp_norm_w, w_ff1, w_ff2, final_norm_w)
    y_sample = trunk(x_sample, attn_norm_w, w_in, w_alpha_fwd, b_alpha_fwd, w_alpha_bwd, b_alpha_bwd,
                     gla_norm_w, ret_norm_w, ret_norm_b, w_out, mlp_norm_w, w_ff1, w_ff2, final_norm_w)
    return (y_prompt, y_sample)
```

```python
import functools

import numpy as np
import jax
import jax.numpy as jnp
from jax import lax
from jax.experimental import pallas as pl
from jax.experimental.pallas import tpu as pltpu

F32 = jnp.float32
BF16 = jnp.bfloat16

D_MODEL = 1024
GLA_HEADS = 4
GLA_DK = 64
GLA_DV = 128
GLA_QK = GLA_HEADS * GLA_DK
GLA_V = GLA_HEADS * GLA_DV
GLA_LOWRANK = 16
GLA_NORMALIZER = 16.0
LOG_GATE_MIN = -1.0
RET_HEADS = 4
RET_DK = 128
RET_DV = 128
RET_QK = RET_HEADS * RET_DK
RET_V = RET_HEADS * RET_DV
D_FF = 4 * D_MODEL
ROPE_BASE = 10000.0
EPS = 1e-6

LANES = 128
BLOCK = 256
CHUNK = 64
N_CHUNKS = BLOCK // CHUNK
FF_TILE = 1024
VMEM_LIMIT_BYTES = 56 * 1024 * 1024

GQ, GK, GV, GG = 0, 256, 512, 1024
RQ, RK, RV, RG = 1536, 2048, 2560, 3072
P_COLS = 3584
GA = P_COLS
W_IN_COLS = P_COLS + LANES

NT = (((1,), (1,)), ((), ()))
TN = (((0,), (0,)), ((), ()))


def _dot(a, b, dims=None):
    if dims is None:
        return jnp.dot(a, b, preferred_element_type=F32)
    return lax.dot_general(a, b, dims, preferred_element_type=F32)


def _rms(x, w):
    return x * lax.rsqrt(jnp.mean(x * x, axis=-1, keepdims=True) + EPS) * w


def _silu(g):
    return g * jax.nn.sigmoid(g)


def _split3(a):
    h1 = a.astype(BF16)
    r1 = a - h1.astype(F32)
    h2 = r1.astype(BF16)
    r2 = r1 - h2.astype(F32)
    return h1, h2, r2.astype(BF16)


def _gla_state_increment(v, ke):
    full = _dot(v, ke, TN)
    lane_head = lax.broadcasted_iota(jnp.int32, (GLA_DV, GLA_QK), 1) // GLA_DK
    inc = jnp.zeros((GLA_DV, GLA_QK), F32)
    for h in range(GLA_HEADS):
        inc = jnp.where(lane_head == h, full[h * GLA_DV:(h + 1) * GLA_DV, :], inc)
    return inc


def _proj_kernel(x_ref, cos_ref, sin_ref, anw_ref, win_ref, wal_ref, bal_ref, eb_ref,
                 proj_ref, cg_ref, ug_ref, ur_ref, sg_ref, sr_ref, *, ret_block_decay):
    @pl.when(pl.program_id(1) == 0)
    def _():
        sg_ref[...] = jnp.zeros_like(sg_ref)
        sr_ref[...] = jnp.zeros_like(sr_ref)

    x = x_ref[...]
    h = _rms(x, anw_ref[...]).astype(BF16)

    def proj(lo, width):
        return _dot(h, win_ref[:, lo:lo + width])

    proj_ref[:, GQ:GQ + GLA_QK] = proj(GQ, GLA_QK).astype(BF16)
    gk = proj(GK, GLA_QK)
    proj_ref[:, GK:GK + GLA_QK] = gk.astype(BF16)
    gv = proj(GV, GLA_V).astype(BF16)
    proj_ref[:, GV:GV + GLA_V] = gv
    proj_ref[:, GG:GG + GLA_V] = proj(GG, GLA_V).astype(BF16)
    proj_ref[:, RG:RG + RET_V] = proj(RG, RET_V).astype(BF16)

    ga = proj(GA, LANES).astype(BF16)
    z = _dot(ga, wal_ref[...]) + bal_ref[...]
    la = jnp.maximum(jax.nn.log_sigmoid(z) * (1.0 / GLA_NORMALIZER), LOG_GATE_MIN)
    row = lax.broadcasted_iota(jnp.int32, (BLOCK, BLOCK), 0)
    col = lax.broadcasted_iota(jnp.int32, (BLOCK, BLOCK), 1)
    tri_f = jnp.where(col <= row, 1.0, 0.0).astype(BF16)
    tri_b = jnp.where(col >= row, 1.0, 0.0).astype(BF16)
    bf = sum(_dot(tri_f, p) for p in _split3(la[:, :GLA_QK]))
    cb = sum(_dot(tri_b, p) for p in _split3(la[:, GLA_QK:]))
    cg_ref[:, :GLA_QK] = bf
    cg_ref[:, GLA_QK:] = cb

    ug_ref[...] = sg_ref[...]
    ctot = cb[0:1, :]
    kb = (gk * jnp.exp(ctot - cb)).astype(BF16)
    sg_ref[...] = sg_ref[...] * jnp.exp(ctot) + _gla_state_increment(gv, kb)

    cos = cos_ref[...]
    sin = sin_ref[...]
    ur_ref[...] = sr_ref[...]
    for hh in range(RET_HEADS):
        lo = hh * RET_DK
        q = proj(RQ + lo, RET_DK)
        k = proj(RK + lo, RET_DK)
        q = q * cos + pltpu.roll(q, RET_DK // 2, axis=1) * sin
        k = k * cos + pltpu.roll(k, RET_DK // 2, axis=1) * sin
        proj_ref[:, RQ + lo:RQ + lo + RET_DK] = q.astype(BF16)
        proj_ref[:, RK + lo:RK + lo + RET_DK] = k.astype(BF16)
        v = proj(RV + lo, RET_DV).astype(BF16)
        proj_ref[:, RV + lo:RV + lo + RET_DV] = v
        kd = (k * eb_ref[:, lo:lo + RET_DK]).astype(BF16)
        sr_ref[hh] = sr_ref[hh] * ret_block_decay[hh] + _dot(kd, v, TN)


def _mix_kernel(x_ref, proj_ref, cg_ref, ug_ref, ur_ref, dmat_ref, df_ref, db_ref, ef_ref,
                gnw_ref, rnw_ref, rnb_ref, wout_ref, mnw_ref, w1_ref, w2_ref, fnw_ref,
                y_ref, fg_ref, fr_ref, *, ret_block_decay, final_norm):
    @pl.when(pl.program_id(1) == 0)
    def _():
        fg_ref[...] = jnp.zeros_like(fg_ref)
        fr_ref[...] = jnp.zeros_like(fr_ref)

    ri = lax.broadcasted_iota(jnp.int32, (CHUNK, CHUNK), 0)
    ci = lax.broadcasted_iota(jnp.int32, (CHUNK, CHUNK), 1)

    def gla_chunk(a, cum, state, forward):
        r0 = a * CHUNK
        e = jnp.exp(cum)
        e_edge = e[CHUNK - 1:CHUNK, :] if forward else e[0:1, :]
        q = proj_ref[r0:r0 + CHUNK, GQ:GQ + GLA_QK].astype(F32)
        k = proj_ref[r0:r0 + CHUNK, GK:GK + GLA_QK].astype(F32)
        v = proj_ref[r0:r0 + CHUNK, GV:GV + GLA_V]
        qd = (q * e).astype(BF16)
        ki = k * jnp.exp(-cum)
        ke = (ki * e_edge).astype(BF16)
        ki = ki.astype(BF16)
        sb = state.astype(BF16)
        keep = (ci <= ri) if forward else (ci > ri)
        outs = []
        for h in range(GLA_HEADS):
            qh = qd[:, h * GLA_DK:(h + 1) * GLA_DK]
            s = _dot(qh, ki[:, h * GLA_DK:(h + 1) * GLA_DK], NT)
            p = jnp.where(keep, s, 0.0).astype(BF16)
            outs.append(_dot(p, v[:, h * GLA_DV:(h + 1) * GLA_DV])
                        + _dot(qh, sb[:, h * GLA_DK:(h + 1) * GLA_DK], NT))
        return jnp.concatenate(outs, axis=1), (v, ke, e_edge)

    state = fg_ref[...]
    fwd = []
    for a in range(N_CHUNKS):
        r0 = a * CHUNK
        cum = cg_ref[r0:r0 + CHUNK, 0:GLA_QK]
        if a > 0:
            cum = cum - cg_ref[r0 - 1:r0, 0:GLA_QK]
        o, (v, ke, e_edge) = gla_chunk(a, cum, state, True)
        fwd.append(o)
        state = state * e_edge + _gla_state_increment(v, ke)
    fg_ref[...] = state
    state = ug_ref[...]
    bwd = [None] * N_CHUNKS
    for a in reversed(range(N_CHUNKS)):
        r0 = a * CHUNK
        cum = cg_ref[r0:r0 + CHUNK, GLA_QK:2 * GLA_QK]
        if a < N_CHUNKS - 1:
            cum = cum - cg_ref[r0 + CHUNK:r0 + CHUNK + 1, GLA_QK:2 * GLA_QK]
        o, (v, ke, e_edge) = gla_chunk(a, cum, state, False)
        bwd[a] = o
        if a > 0:
            state = state * e_edge + _gla_state_increment(v, ke)
    o_gla = jnp.concatenate([f + b for f, b in zip(fwd, bwd)], axis=0)

    mixed = []
    gnw = gnw_ref[...]
    for h in range(GLA_HEADS):
        lo = h * GLA_DV
        oh = _rms(o_gla[:, lo:lo + GLA_DV], gnw)
        mixed.append(oh * _silu(proj_ref[:, GG + lo:GG + lo + GLA_DV].astype(F32)))

    for h in range(RET_HEADS):
        lo = h * RET_DK
        q = proj_ref[:, RQ + lo:RQ + lo + RET_DK]
        k = proj_ref[:, RK + lo:RK + lo + RET_DK]
        v = proj_ref[:, RV + lo:RV + lo + RET_DV]
        p = (_dot(q, k, NT) * dmat_ref[h]).astype(BF16)
        qf = q.astype(F32)
        q_fwd = (qf * df_ref[:, lo:lo + RET_DK]).astype(BF16)
        q_bwd = (qf * db_ref[:, lo:lo + RET_DK]).astype(BF16)
        o = (_dot(p, v) + _dot(q_fwd, fr_ref[h].astype(BF16))
             + _dot(q_bwd, ur_ref[h].astype(BF16)))
        kd = (k.astype(F32) * ef_ref[:, lo:lo + RET_DK]).astype(BF16)
        fr_ref[h] = fr_ref[h] * ret_block_decay[h] + _dot(kd, v, TN)
        mu = jnp.mean(o, axis=-1, keepdims=True)
        oc = o - mu
        var = jnp.mean(oc * oc, axis=-1, keepdims=True)
        on = oc * lax.rsqrt(var + EPS) * rnw_ref[:, lo:lo + RET_DV] + rnb_ref[:, lo:lo + RET_DV]
        mixed.append(on * _silu(proj_ref[:, RG + lo:RG + lo + RET_DV].astype(F32)))

    mixed = jnp.concatenate(mixed, axis=1).astype(BF16)
    x1 = x_ref[...] + _dot(mixed, wout_ref[...])
    h2 = _rms(x1, mnw_ref[...]).astype(BF16)
    acc = x1
    for f in range(D_FF // FF_TILE):
        ff = jnp.maximum(_dot(h2, w1_ref[:, f * FF_TILE:(f + 1) * FF_TILE]), 0.0)
        acc = acc + _dot((ff * ff).astype(BF16), w2_ref[f * FF_TILE:(f + 1) * FF_TILE, :])
    y_ref[...] = _rms(acc, fnw_ref[...]) if final_norm else acc


def _const_spec(shape):
    zeros = (0,) * len(shape)
    return pl.BlockSpec(shape, lambda b, n: zeros, pipeline_mode=pl.Buffered(1))


def _retention_tables():
    log_gamma = np.log1p(-np.power(2.0, -5.0 - np.arange(RET_HEADS, dtype=np.float64)))
    i = np.arange(BLOCK, dtype=np.float64)
    scale = RET_DK ** -0.5
    dist = np.abs(i[:, None] - i[None, :])
    dmat = np.exp(log_gamma[:, None, None] * dist[None]) * scale

    def expand(t):
        return np.repeat(t, RET_DK, axis=1).astype(np.float32)

    df = expand(np.exp(np.outer(i + 1.0, log_gamma)) * scale)
    db = expand(np.exp(np.outer(BLOCK - i, log_gamma)) * scale)
    ef = expand(np.exp(np.outer(BLOCK - 1.0 - i, log_gamma)))
    eb = expand(np.exp(np.outer(i, log_gamma)))
    block_decay = tuple(float(g) for g in np.exp(BLOCK * log_gamma))
    return dmat.astype(np.float32), df, db, ef, eb, block_decay


def _rotary_tables(seq_len):
    half = RET_DK // 2
    inv_freq = jnp.power(ROPE_BASE, -jnp.arange(half, dtype=F32) / half)
    ang = jnp.arange(seq_len, dtype=F32)[:, None] * inv_freq[None, :]
    cos, sin = jnp.cos(ang), jnp.sin(ang)
    return jnp.concatenate([cos, cos], axis=1), jnp.concatenate([-sin, sin], axis=1)


def _prepare_layer(attn_norm_w, w_in, w_alpha_fwd, b_alpha_fwd, w_alpha_bwd, b_alpha_bwd,
                   gla_norm_w, ret_norm_w, ret_norm_b, w_out, mlp_norm_w, w_ff1, w_ff2):
    o = np.cumsum((0, GLA_QK, GLA_QK, GLA_V, GLA_V, GLA_LOWRANK, GLA_LOWRANK,
                   RET_QK, RET_QK, RET_V, RET_V))
    gq, gk, gv, gg, ga_f, ga_b, rq, rk, rv, rg = (w_in[:, o[i]:o[i + 1]] for i in range(10))
    pad = jnp.zeros((D_MODEL, LANES - 2 * GLA_LOWRANK), w_in.dtype)
    w_in_p = jnp.concatenate([gq * (GLA_DK ** -0.5), gk, gv, gg, rq, rk, rv, rg, ga_f, ga_b, pad],
                             axis=1).astype(BF16)
    w_al = jnp.zeros((LANES, 2 * GLA_QK), F32)
    w_al = w_al.at[:GLA_LOWRANK, :GLA_QK].set(w_alpha_fwd)
    w_al = w_al.at[GLA_LOWRANK:2 * GLA_LOWRANK, GLA_QK:].set(w_alpha_bwd)
    b_al = jnp.concatenate([b_alpha_fwd, b_alpha_bwd])[None, :].astype(F32)
    return dict(
        anw=attn_norm_w[None, :].astype(F32), w_in=w_in_p, w_al=w_al.astype(BF16), b_al=b_al,
        gnw=gla_norm_w[None, :].astype(F32), rnw=ret_norm_w[None, :].astype(F32),
        rnb=ret_norm_b[None, :].astype(F32), w_out=w_out.astype(BF16),
        mnw=mlp_norm_w[None, :].astype(F32), w1=w_ff1.astype(BF16), w2=w_ff2.astype(BF16))


def _layer(x, lw, fnw, final_norm, interpret=False):
    batch, seq_len, _ = x.shape
    assert seq_len % BLOCK == 0
    nb = seq_len // BLOCK
    dmat, df, db, ef, eb, block_decay = _retention_tables()
    cos, sin = _rotary_tables(seq_len)
    params = pltpu.CompilerParams(dimension_semantics=("arbitrary", "arbitrary"),
                                  vmem_limit_bytes=VMEM_LIMIT_BYTES)

    def rev(b, n):
        return (b, nb - 1 - n, 0)

    def fwd(b, n):
        return (b, n, 0)

    proj, cg, ug, ur = pl.pallas_call(
        functools.partial(_proj_kernel, ret_block_decay=block_decay),
        grid=(batch, nb),
        in_specs=[
            pl.BlockSpec((None, BLOCK, D_MODEL), rev),
            pl.BlockSpec((BLOCK, LANES), lambda b, n: (nb - 1 - n, 0)),
            pl.BlockSpec((BLOCK, LANES), lambda b, n: (nb - 1 - n, 0)),
            _const_spec((1, D_MODEL)),
            _const_spec((D_MODEL, W_IN_COLS)),
            _const_spec((LANES, 2 * GLA_QK)),
            _const_spec((1, 2 * GLA_QK)),
            _const_spec((BLOCK, RET_QK)),
        ],
        out_specs=[
            pl.BlockSpec((None, BLOCK, P_COLS), rev),
            pl.BlockSpec((None, BLOCK, 2 * GLA_QK), rev),
            pl.BlockSpec((None, None, GLA_DV, GLA_QK), lambda b, n: (b, nb - 1 - n, 0, 0)),
            pl.BlockSpec((None, None, RET_HEADS, RET_DK, RET_DV), lambda b, n: (b, nb - 1 - n, 0, 0, 0)),
        ],
        out_shape=[
            jax.ShapeDtypeStruct((batch, seq_len, P_COLS), BF16),
            jax.ShapeDtypeStruct((batch, seq_len, 2 * GLA_QK), F32),
            jax.ShapeDtypeStruct((batch, nb, GLA_DV, GLA_QK), F32),
            jax.ShapeDtypeStruct((batch, nb, RET_HEADS, RET_DK, RET_DV), F32),
        ],
        scratch_shapes=[pltpu.VMEM((GLA_DV, GLA_QK), F32),
                        pltpu.VMEM((RET_HEADS, RET_DK, RET_DV), F32)],
        compiler_params=params,
        name="proj_bwd_state",
        interpret=interpret,
    )(x, cos, sin, lw["anw"], lw["w_in"], lw["w_al"], lw["b_al"], eb)

    return pl.pallas_call(
        functools.partial(_mix_kernel, ret_block_decay=block_decay, final_norm=final_norm),
        grid=(batch, nb),
        in_specs=[
            pl.BlockSpec((None, BLOCK, D_MODEL), fwd),
            pl.BlockSpec((None, BLOCK, P_COLS), fwd),
            pl.BlockSpec((None, BLOCK, 2 * GLA_QK), fwd),
            pl.BlockSpec((None, None, GLA_DV, GLA_QK), lambda b, n: (b, n, 0, 0)),
            pl.BlockSpec((None, None, RET_HEADS, RET_DK, RET_DV), lambda b, n: (b, n, 0, 0, 0)),
            _const_spec((RET_HEADS, BLOCK, BLOCK)),
            _const_spec((BLOCK, RET_QK)),
            _const_spec((BLOCK, RET_QK)),
            _const_spec((BLOCK, RET_QK)),
            _const_spec((1, GLA_DV)),
            _const_spec((1, RET_V)),
            _const_spec((1, RET_V)),
            _const_spec((D_MODEL, D_MODEL)),
            _const_spec((1, D_MODEL)),
            _const_spec((D_MODEL, D_FF)),
            _const_spec((D_FF, D_MODEL)),
            _const_spec((1, D_MODEL)),
        ],
        out_specs=pl.BlockSpec((None, BLOCK, D_MODEL), fwd),
        out_shape=jax.ShapeDtypeStruct((batch, seq_len, D_MODEL), F32),
        scratch_shapes=[pltpu.VMEM((GLA_DV, GLA_QK), F32),
                        pltpu.VMEM((RET_HEADS, RET_DK, RET_DV), F32)],
        compiler_params=params,
        name="mix_mlp",
        interpret=interpret,
    )(x, proj, cg, ug, ur, dmat, df, db, ef, lw["gnw"], lw["rnw"], lw["rnb"], lw["w_out"],
      lw["mnw"], lw["w1"], lw["w2"], fnw)


def _trunk(x, layers, fnw, interpret=False):
    for i, lw in enumerate(layers):
        x = _layer(x, lw, fnw, final_norm=(i == len(layers) - 1), interpret=interpret)
    return x


def kernel(x_prompt, x_sample, attn_norm_w, w_in, w_alpha_fwd, b_alpha_fwd, w_alpha_bwd, b_alpha_bwd,
           gla_norm_w, ret_norm_w, ret_norm_b, w_out, mlp_norm_w, w_ff1, w_ff2, final_norm_w):
    stacked = (attn_norm_w, w_in, w_alpha_fwd, b_alpha_fwd, w_alpha_bwd, b_alpha_bwd,
               gla_norm_w, ret_norm_w, ret_norm_b, w_out, mlp_norm_w, w_ff1, w_ff2)
    layers = [_prepare_layer(*(w[l] for w in stacked)) for l in range(w_in.shape[0])]
    fnw = final_norm_w[None, :].astype(F32)
    return _trunk(x_prompt, layers, fnw), _trunk(x_sample, layers, fnw)
```

```python
import functools

import numpy as np
import jax
import jax.numpy as jnp
from jax import lax
from jax.experimental import pallas as pl
from jax.experimental.pallas import tpu as pltpu

F32 = jnp.float32
BF16 = jnp.bfloat16

D_MODEL = 1024
GLA_HEADS = 4
GLA_DK = 64
GLA_DV = 128
GLA_QK = GLA_HEADS * GLA_DK
GLA_V = GLA_HEADS * GLA_DV
GLA_LOWRANK = 16
GLA_NORMALIZER = 16.0
LOG_GATE_MIN = -1.0
RET_HEADS = 4
RET_DK = 128
RET_DV = 128
RET_QK = RET_HEADS * RET_DK
RET_V = RET_HEADS * RET_DV
D_FF = 4 * D_MODEL
ROPE_BASE = 10000.0
EPS = 1e-6

LANES = 128
BLOCK = 256
CHUNK = 64
N_CHUNKS = BLOCK // CHUNK
FF_TILE = 512
VMEM_LIMIT_BYTES = 56 * 1024 * 1024

GQ, GK, GV, GG = 0, 256, 512, 1024
RQ, RK, RV, RG = 1536, 2048, 2560, 3072
P_COLS = 3584
GA = P_COLS
W_IN_COLS = P_COLS + LANES

NT = (((1,), (1,)), ((), ()))
TN = (((0,), (0,)), ((), ()))


def _dot(a, b, dims=None):
    if dims is None:
        return jnp.dot(a, b, preferred_element_type=F32)
    return lax.dot_general(a, b, dims, preferred_element_type=F32)


def _rms(x, w):
    return x * lax.rsqrt(jnp.mean(x * x, axis=-1, keepdims=True) + EPS) * w


def _silu(g):
    return g * jax.nn.sigmoid(g)


def _split3(a):
    h1 = a.astype(BF16)
    r1 = a - h1.astype(F32)
    h2 = r1.astype(BF16)
    r2 = r1 - h2.astype(F32)
    return h1, h2, r2.astype(BF16)


def _gla_state_increment(v, ke):
    full = _dot(v, ke, TN)
    lane_head = lax.broadcasted_iota(jnp.int32, (GLA_DV, GLA_QK), 1) // GLA_DK
    inc = jnp.zeros((GLA_DV, GLA_QK), F32)
    for h in range(GLA_HEADS):
        inc = jnp.where(lane_head == h, full[h * GLA_DV:(h + 1) * GLA_DV, :], inc)
    return inc


def _proj_kernel(x_ref, cos_ref, sin_ref, anw_ref, win_ref, wal_ref, bal_ref, eb_ref,
                 proj_ref, cg_ref, ug_ref, ur_ref, sg_ref, sr_ref, *, ret_block_decay):
    @pl.when(pl.program_id(1) == 0)
    def _():
        sg_ref[...] = jnp.zeros_like(sg_ref)
        sr_ref[...] = jnp.zeros_like(sr_ref)

    x = x_ref[...]
    h = _rms(x, anw_ref[...]).astype(BF16)

    def proj(lo, width):
        return _dot(h, win_ref[:, lo:lo + width])

    rg_ga = proj(RG, RET_V + LANES)
    proj_ref[:, RG:RG + RET_V] = rg_ga[:, :RET_V].astype(BF16)
    ga = rg_ga[:, RET_V:].astype(BF16)
    z = _dot(ga, wal_ref[...]) + bal_ref[...]
    la = jnp.maximum(jax.nn.log_sigmoid(z) * (1.0 / GLA_NORMALIZER), LOG_GATE_MIN)
    la_f = _split3(la[:, :GLA_QK])
    la_b = _split3(la[:, GLA_QK:])

    gqk = proj(GQ, 2 * GLA_QK)
    proj_ref[:, GQ:GQ + 2 * GLA_QK] = gqk.astype(BF16)
    gk = gqk[:, GLA_QK:]
    gv = proj(GV, GLA_V).astype(BF16)
    proj_ref[:, GV:GV + GLA_V] = gv
    proj_ref[:, GG:GG + GLA_V] = proj(GG, GLA_V).astype(BF16)
    rq_all = proj(RQ, RET_QK)
    rk_all = proj(RK, RET_QK)
    rv_all = proj(RV, RET_V).astype(BF16)
    proj_ref[:, RV:RV + RET_V] = rv_all

    row = lax.broadcasted_iota(jnp.int32, (BLOCK, BLOCK), 0)
    col = lax.broadcasted_iota(jnp.int32, (BLOCK, BLOCK), 1)
    tri_f = jnp.where(col <= row, 1.0, 0.0).astype(BF16)
    tri_b = jnp.where(col >= row, 1.0, 0.0).astype(BF16)
    bf = sum(_dot(tri_f, p) for p in la_f)
    cb = sum(_dot(tri_b, p) for p in la_b)
    cg_ref[:, :GLA_QK] = bf
    cg_ref[:, GLA_QK:] = cb

    cos = cos_ref[...]
    sin = sin_ref[...]
    ur_ref[...] = sr_ref[...]
    for hh in range(RET_HEADS):
        lo = hh * RET_DK
        q = rq_all[:, lo:lo + RET_DK]
        k = rk_all[:, lo:lo + RET_DK]
        q = q * cos + pltpu.roll(q, RET_DK // 2, axis=1) * sin
        k = k * cos + pltpu.roll(k, RET_DK // 2, axis=1) * sin
        proj_ref[:, RQ + lo:RQ + lo + RET_DK] = q.astype(BF16)
        proj_ref[:, RK + lo:RK + lo + RET_DK] = k.astype(BF16)
        v = rv_all[:, lo:lo + RET_DV]
        kd = (k * eb_ref[:, lo:lo + RET_DK]).astype(BF16)
        sr_ref[hh] = sr_ref[hh] * ret_block_decay[hh] + _dot(kd, v, TN)

    ug_ref[...] = sg_ref[...]
    ctot = cb[0:1, :]
    kb = (gk * jnp.exp(ctot - cb)).astype(BF16)
    sg_ref[...] = sg_ref[...] * jnp.exp(ctot) + _gla_state_increment(gv, kb)


def _mix_kernel(xprev_ref, proj_ref, cg_ref, ug_ref, ur_ref, dmat_ref, df_ref, db_ref, ef_ref,
                gnw_ref, rnw_ref, rnb_ref, wout_ref, mnw_ref, w1_ref, w2_ref, fnw_ref,
                y_ref, fg_ref, fr_ref, mixed_ref, *, blocks_per_seq, ret_block_decay, final_norm):
    g = pl.program_id(0)

    @pl.when(g % blocks_per_seq == 0)
    def _():
        fg_ref[...] = jnp.zeros_like(fg_ref)
        fr_ref[...] = jnp.zeros_like(fr_ref)

    @pl.when(g == 0)
    def _():
        mixed_ref[...] = jnp.zeros_like(mixed_ref)

    dense = {}

    def dense_head():
        x1 = xprev_ref[...] + _dot(mixed_ref[...], wout_ref[...])
        dense["h2"] = _rms(x1, mnw_ref[...]).astype(BF16)
        dense["acc"] = x1

    def dense_tile(f):
        ff = jnp.maximum(_dot(dense["h2"], w1_ref[:, f * FF_TILE:(f + 1) * FF_TILE]), 0.0)
        dense["acc"] = dense["acc"] + _dot((ff * ff).astype(BF16),
                                           w2_ref[f * FF_TILE:(f + 1) * FF_TILE, :])

    def dense_finish():
        y_ref[...] = _rms(dense["acc"], fnw_ref[...]) if final_norm else dense["acc"]

    n_ff = D_FF // FF_TILE
    dense_head()

    def gla_operands(a, forward):
        r0 = a * CHUNK
        if forward:
            cum = cg_ref[r0:r0 + CHUNK, 0:GLA_QK]
            if a > 0:
                cum = cum - cg_ref[r0 - 1:r0, 0:GLA_QK]
        else:
            cum = cg_ref[r0:r0 + CHUNK, GLA_QK:2 * GLA_QK]
            if a < N_CHUNKS - 1:
                cum = cum - cg_ref[r0 + CHUNK:r0 + CHUNK + 1, GLA_QK:2 * GLA_QK]
        e = jnp.exp(cum)
        e_edge = e[CHUNK - 1:CHUNK, :] if forward else e[0:1, :]
        q = proj_ref[r0:r0 + CHUNK, GQ:GQ + GLA_QK].astype(F32)
        k = proj_ref[r0:r0 + CHUNK, GK:GK + GLA_QK].astype(F32)
        ki = k * jnp.exp(-cum)
        return dict(qd=(q * e).astype(BF16), ki=ki.astype(BF16), ke=(ki * e_edge).astype(BF16),
                    e_edge=e_edge, v=proj_ref[r0:r0 + CHUNK, GV:GV + GLA_V])

    sweeps = {True: list(range(N_CHUNKS)), False: list(reversed(range(N_CHUNKS)))}
    ops = {(fw, a): gla_operands(a, fw) for fw in (True, False) for a in sweeps[fw]}

    ret = []
    for h in range(RET_HEADS):
        lo = h * RET_DK
        q = proj_ref[:, RQ + lo:RQ + lo + RET_DK]
        k = proj_ref[:, RK + lo:RK + lo + RET_DK]
        qf = q.astype(F32)
        ret.append(dict(
            q=q, k=k, v=proj_ref[:, RV + lo:RV + lo + RET_DV],
            q_fwd=(qf * df_ref[:, lo:lo + RET_DK]).astype(BF16),
            q_bwd=(qf * db_ref[:, lo:lo + RET_DK]).astype(BF16),
            kd=(k.astype(F32) * ef_ref[:, lo:lo + RET_DK]).astype(BF16)))

    def head_cols(t, h, width):
        return t[:, h * width:(h + 1) * width]

    for key, op in ops.items():
        fw, a = key
        if fw or a > 0:
            op["inc"] = _gla_state_increment(op["v"], op["ke"])
        op["s"] = [_dot(head_cols(op["qd"], h, GLA_DK), head_cols(op["ki"], h, GLA_DK), NT)
                   for h in range(GLA_HEADS)]
    for h, r in enumerate(ret):
        r["s"] = _dot(r["q"], r["k"], NT)
        r["inter"] = (_dot(r["q_fwd"], fr_ref[h].astype(BF16))
                      + _dot(r["q_bwd"], ur_ref[h].astype(BF16)))
        fr_ref[h] = fr_ref[h] * ret_block_decay[h] + _dot(r["kd"], r["v"], TN)

    for f in range(n_ff // 2):
        dense_tile(f)

    ri = lax.broadcasted_iota(jnp.int32, (CHUNK, CHUNK), 0)
    ci = lax.broadcasted_iota(jnp.int32, (CHUNK, CHUNK), 1)
    for fw in (True, False):
        state = fg_ref[...] if fw else ug_ref[...]
        keep = (ci <= ri) if fw else (ci > ri)
        for a in sweeps[fw]:
            op = ops[(fw, a)]
            op["state"] = state.astype(BF16)
            op["p"] = [jnp.where(keep, s, 0.0).astype(BF16) for s in op["s"]]
            if "inc" in op:
                state = state * op["e_edge"] + op["inc"]
        if fw:
            fg_ref[...] = state
    for h, r in enumerate(ret):
        r["p"] = (r["s"] * dmat_ref[h]).astype(BF16)

    for op in ops.values():
        op["o"] = jnp.concatenate(
            [_dot(op["p"][h], head_cols(op["v"], h, GLA_DV))
             + _dot(head_cols(op["qd"], h, GLA_DK), head_cols(op["state"], h, GLA_DK), NT)
             for h in range(GLA_HEADS)], axis=1)
    for r in ret:
        r["o"] = _dot(r["p"], r["v"]) + r["inter"]

    for f in range(n_ff // 2, n_ff):
        dense_tile(f)

    o_gla = jnp.concatenate([ops[(True, a)]["o"] + ops[(False, a)]["o"] for a in range(N_CHUNKS)],
                            axis=0)
    mixed = []
    gnw = gnw_ref[...]
    for h in range(GLA_HEADS):
        lo = h * GLA_DV
        oh = _rms(o_gla[:, lo:lo + GLA_DV], gnw)
        mixed.append(oh * _silu(proj_ref[:, GG + lo:GG + lo + GLA_DV].astype(F32)))
    for h, r in enumerate(ret):
        lo = h * RET_DV
        o = r["o"]
        mu = jnp.mean(o, axis=-1, keepdims=True)
        oc = o - mu
        var = jnp.mean(oc * oc, axis=-1, keepdims=True)
        on = oc * lax.rsqrt(var + EPS) * rnw_ref[:, lo:lo + RET_DV] + rnb_ref[:, lo:lo + RET_DV]
        mixed.append(on * _silu(proj_ref[:, RG + lo:RG + lo + RET_DV].astype(F32)))
    mixed_ref[...] = jnp.concatenate(mixed, axis=1).astype(BF16)
    dense_finish()


def _const_spec(shape):
    zeros = (0,) * len(shape)
    return pl.BlockSpec(shape, lambda *_: zeros, pipeline_mode=pl.Buffered(1))


def _retention_tables():
    log_gamma = np.log1p(-np.power(2.0, -5.0 - np.arange(RET_HEADS, dtype=np.float64)))
    i = np.arange(BLOCK, dtype=np.float64)
    scale = RET_DK ** -0.5
    dist = np.abs(i[:, None] - i[None, :])
    dmat = np.exp(log_gamma[:, None, None] * dist[None]) * scale

    def expand(t):
        return np.repeat(t, RET_DK, axis=1).astype(np.float32)

    df = expand(np.exp(np.outer(i + 1.0, log_gamma)) * scale)
    db = expand(np.exp(np.outer(BLOCK - i, log_gamma)) * scale)
    ef = expand(np.exp(np.outer(BLOCK - 1.0 - i, log_gamma)))
    eb = expand(np.exp(np.outer(i, log_gamma)))
    block_decay = tuple(float(g) for g in np.exp(BLOCK * log_gamma))
    return dmat.astype(np.float32), df, db, ef, eb, block_decay


def _rotary_tables(seq_len):
    half = RET_DK // 2
    inv_freq = jnp.power(ROPE_BASE, -jnp.arange(half, dtype=F32) / half)
    ang = jnp.arange(seq_len, dtype=F32)[:, None] * inv_freq[None, :]
    cos, sin = jnp.cos(ang), jnp.sin(ang)
    return jnp.concatenate([cos, cos], axis=1), jnp.concatenate([-sin, sin], axis=1)


def _prepare_layer(attn_norm_w, w_in, w_alpha_fwd, b_alpha_fwd, w_alpha_bwd, b_alpha_bwd,
                   gla_norm_w, ret_norm_w, ret_norm_b, w_out, mlp_norm_w, w_ff1, w_ff2):
    o = np.cumsum((0, GLA_QK, GLA_QK, GLA_V, GLA_V, GLA_LOWRANK, GLA_LOWRANK,
                   RET_QK, RET_QK, RET_V, RET_V))
    gq, gk, gv, gg, ga_f, ga_b, rq, rk, rv, rg = (w_in[:, o[i]:o[i + 1]] for i in range(10))
    pad = jnp.zeros((D_MODEL, LANES - 2 * GLA_LOWRANK), w_in.dtype)
    w_in_p = jnp.concatenate([gq * (GLA_DK ** -0.5), gk, gv, gg, rq, rk, rv, rg, ga_f, ga_b, pad],
                             axis=1).astype(BF16)
    w_al = jnp.zeros((LANES, 2 * GLA_QK), F32)
    w_al = w_al.at[:GLA_LOWRANK, :GLA_QK].set(w_alpha_fwd)
    w_al = w_al.at[GLA_LOWRANK:2 * GLA_LOWRANK, GLA_QK:].set(w_alpha_bwd)
    b_al = jnp.concatenate([b_alpha_fwd, b_alpha_bwd])[None, :].astype(F32)
    return dict(
        anw=attn_norm_w[None, :].astype(F32), w_in=w_in_p, w_al=w_al.astype(BF16), b_al=b_al,
        gnw=gla_norm_w[None, :].astype(F32), rnw=ret_norm_w[None, :].astype(F32),
        rnb=ret_norm_b[None, :].astype(F32), w_out=w_out.astype(BF16),
        mnw=mlp_norm_w[None, :].astype(F32), w1=w_ff1.astype(BF16), w2=w_ff2.astype(BF16))


def _layer(x, lw, fnw, final_norm, interpret=False):
    batch, seq_len, _ = x.shape
    assert seq_len % BLOCK == 0
    nb = seq_len // BLOCK
    dmat, df, db, ef, eb, block_decay = _retention_tables()
    cos, sin = _rotary_tables(seq_len)
    params = pltpu.CompilerParams(dimension_semantics=("arbitrary", "arbitrary"),
                                  vmem_limit_bytes=VMEM_LIMIT_BYTES)

    def rev(b, n):
        return (b, nb - 1 - n, 0)

    proj, cg, ug, ur = pl.pallas_call(
        functools.partial(_proj_kernel, ret_block_decay=block_decay),
        grid=(batch, nb),
        in_specs=[
            pl.BlockSpec((None, BLOCK, D_MODEL), rev),
            pl.BlockSpec((BLOCK, LANES), lambda b, n: (nb - 1 - n, 0)),
            pl.BlockSpec((BLOCK, LANES), lambda b, n: (nb - 1 - n, 0)),
            _const_spec((1, D_MODEL)),
            _const_spec((D_MODEL, W_IN_COLS)),
            _const_spec((LANES, 2 * GLA_QK)),
            _const_spec((1, 2 * GLA_QK)),
            _const_spec((BLOCK, RET_QK)),
        ],
        out_specs=[
            pl.BlockSpec((None, BLOCK, P_COLS), rev),
            pl.BlockSpec((None, BLOCK, 2 * GLA_QK), rev),
            pl.BlockSpec((None, None, GLA_DV, GLA_QK), lambda b, n: (b, nb - 1 - n, 0, 0)),
            pl.BlockSpec((None, None, RET_HEADS, RET_DK, RET_DV), lambda b, n: (b, nb - 1 - n, 0, 0, 0)),
        ],
        out_shape=[
            jax.ShapeDtypeStruct((batch, seq_len, P_COLS), BF16),
            jax.ShapeDtypeStruct((batch, seq_len, 2 * GLA_QK), F32),
            jax.ShapeDtypeStruct((batch, nb, GLA_DV, GLA_QK), F32),
            jax.ShapeDtypeStruct((batch, nb, RET_HEADS, RET_DK, RET_DV), F32),
        ],
        scratch_shapes=[pltpu.VMEM((GLA_DV, GLA_QK), F32),
                        pltpu.VMEM((RET_HEADS, RET_DK, RET_DV), F32)],
        compiler_params=params,
        name="proj_bwd_state",
        interpret=interpret,
    )(x, cos, sin, lw["anw"], lw["w_in"], lw["w_al"], lw["b_al"], eb)

    total = batch * nb

    def cur(g):
        blk = jnp.minimum(g, total - 1)
        return blk // nb, blk % nb

    def prev(g):
        blk = jnp.maximum(g - 1, 0)
        return blk // nb, blk % nb

    return pl.pallas_call(
        functools.partial(_mix_kernel, blocks_per_seq=nb, ret_block_decay=block_decay,
                          final_norm=final_norm),
        grid=(total + 1,),
        in_specs=[
            pl.BlockSpec((None, BLOCK, D_MODEL), lambda g: (*prev(g), 0)),
            pl.BlockSpec((None, BLOCK, P_COLS), lambda g: (*cur(g), 0)),
            pl.BlockSpec((None, BLOCK, 2 * GLA_QK), lambda g: (*cur(g), 0)),
            pl.BlockSpec((None, None, GLA_DV, GLA_QK), lambda g: (*cur(g), 0, 0)),
            pl.BlockSpec((None, None, RET_HEADS, RET_DK, RET_DV), lambda g: (*cur(g), 0, 0, 0)),
            _const_spec((RET_HEADS, BLOCK, BLOCK)),
            _const_spec((BLOCK, RET_QK)),
            _const_spec((BLOCK, RET_QK)),
            _const_spec((BLOCK, RET_QK)),
            _const_spec((1, GLA_DV)),
            _const_spec((1, RET_V)),
            _const_spec((1, RET_V)),
            _const_spec((D_MODEL, D_MODEL)),
            _const_spec((1, D_MODEL)),
            _const_spec((D_MODEL, D_FF)),
            _const_spec((D_FF, D_MODEL)),
            _const_spec((1, D_MODEL)),
        ],
        out_specs=pl.BlockSpec((None, BLOCK, D_MODEL), lambda g: (*prev(g), 0)),
        out_shape=jax.ShapeDtypeStruct((batch, seq_len, D_MODEL), F32),
        scratch_shapes=[pltpu.VMEM((GLA_DV, GLA_QK), F32),
                        pltpu.VMEM((RET_HEADS, RET_DK, RET_DV), F32),
                        pltpu.VMEM((BLOCK, D_MODEL), BF16)],
        compiler_params=pltpu.CompilerParams(dimension_semantics=("arbitrary",),
                                             vmem_limit_bytes=VMEM_LIMIT_BYTES),
        name="mix_mlp",
        interpret=interpret,
    )(x, proj, cg, ug, ur, dmat, df, db, ef, lw["gnw"], lw["rnw"], lw["rnb"], lw["w_out"],
      lw["mnw"], lw["w1"], lw["w2"], fnw)


def _trunk(x, layers, fnw, interpret=False):
    for i, lw in enumerate(layers):
        x = _layer(x, lw, fnw, final_norm=(i == len(layers) - 1), interpret=interpret)
    return x


def kernel(x_prompt, x_sample, attn_norm_w, w_in, w_alpha_fwd, b_alpha_fwd, w_alpha_bwd, b_alpha_bwd,
           gla_norm_w, ret_norm_w, ret_norm_b, w_out, mlp_norm_w, w_ff1, w_ff2, final_norm_w):
    stacked = (attn_norm_w, w_in, w_alpha_fwd, b_alpha_fwd, w_alpha_bwd, b_alpha_bwd,
               gla_norm_w, ret_norm_w, ret_norm_b, w_out, mlp_norm_w, w_ff1, w_ff2)
    layers = [_prepare_layer(*(w[l] for w in stacked)) for l in range(w_in.shape[0])]
    fnw = final_norm_w[None, :].astype(F32)
    return _trunk(x_prompt, layers, fnw), _trunk(x_sample, layers, fnw)
```

```python
import functools

import numpy as np
import jax
import jax.numpy as jnp
from jax import lax
from jax.experimental import pallas as pl
from jax.experimental.pallas import tpu as pltpu

F32 = jnp.float32
BF16 = jnp.bfloat16

D_MODEL = 1024
GLA_HEADS = 4
GLA_DK = 64
GLA_DV = 128
GLA_QK = GLA_HEADS * GLA_DK
GLA_V = GLA_HEADS * GLA_DV
GLA_LOWRANK = 16
GLA_NORMALIZER = 16.0
LOG_GATE_MIN = -1.0
RET_HEADS = 4
RET_DK = 128
RET_DV = 128
RET_QK = RET_HEADS * RET_DK
RET_V = RET_HEADS * RET_DV
D_FF = 4 * D_MODEL
ROPE_BASE = 10000.0
EPS = 1e-6

LANES = 128
BLOCK = 256
CHUNK = 64
N_CHUNKS = BLOCK // CHUNK
FF_TILE = 512
VMEM_LIMIT_BYTES = 56 * 1024 * 1024

GQ, GK, GV, GG = 0, 256, 512, 1024
RQ, RK, RV, RG = 1536, 2048, 2560, 3072
P_COLS = 3584
GA = P_COLS
W_IN_COLS = P_COLS + LANES

NT = (((1,), (1,)), ((), ()))
TN = (((0,), (0,)), ((), ()))


def _dot(a, b, dims=None):
    if dims is None:
        return jnp.dot(a, b, preferred_element_type=F32)
    return lax.dot_general(a, b, dims, preferred_element_type=F32)


def _rms(x, w):
    return x * lax.rsqrt(jnp.mean(x * x, axis=-1, keepdims=True) + EPS) * w


def _silu(g):
    return g * jax.nn.sigmoid(g)


def _split3(a):
    h1 = a.astype(BF16)
    r1 = a - h1.astype(F32)
    h2 = r1.astype(BF16)
    r2 = r1 - h2.astype(F32)
    return h1, h2, r2.astype(BF16)


def _gla_state_increment(ke, v):
    blocks = []
    for pair in range(GLA_HEADS // 2):
        full = _dot(ke[:, pair * LANES:(pair + 1) * LANES],
                    v[:, pair * 2 * GLA_DV:(pair + 1) * 2 * GLA_DV], TN)
        blocks += [full[:GLA_DK, :GLA_DV], full[GLA_DK:, GLA_DV:]]
    return jnp.concatenate(blocks, axis=0)


def _as_column(row):
    return jnp.broadcast_to(row, (LANES, GLA_QK)).T


def _proj_kernel(x_ref, cos_ref, sin_ref, anw_ref, win_ref, wal_ref, bal_ref, eb_ref,
                 proj_ref, cg_ref, ug_ref, ur_ref, sg_ref, sr_ref, *, ret_block_decay):
    @pl.when(pl.program_id(1) == 0)
    def _():
        sg_ref[...] = jnp.zeros_like(sg_ref)
        sr_ref[...] = jnp.zeros_like(sr_ref)

    x = x_ref[...]
    h = _rms(x, anw_ref[...]).astype(BF16)

    def proj(lo, width):
        return _dot(h, win_ref[:, lo:lo + width])

    rg_ga = proj(RG, RET_V + LANES)
    proj_ref[:, RG:RG + RET_V] = rg_ga[:, :RET_V].astype(BF16)
    ga = rg_ga[:, RET_V:].astype(BF16)
    z = _dot(ga, wal_ref[...]) + bal_ref[...]
    la = jnp.maximum(jax.nn.log_sigmoid(z) * (1.0 / GLA_NORMALIZER), LOG_GATE_MIN)
    la_f = _split3(la[:, :GLA_QK])
    la_b = _split3(la[:, GLA_QK:])

    gqk = proj(GQ, 2 * GLA_QK)
    proj_ref[:, GQ:GQ + 2 * GLA_QK] = gqk.astype(BF16)
    gk = gqk[:, GLA_QK:]
    gv = proj(GV, GLA_V).astype(BF16)
    proj_ref[:, GV:GV + GLA_V] = gv
    proj_ref[:, GG:GG + GLA_V] = proj(GG, GLA_V).astype(BF16)
    rq_all = proj(RQ, RET_QK)
    rk_all = proj(RK, RET_QK)
    rv_all = proj(RV, RET_V).astype(BF16)
    proj_ref[:, RV:RV + RET_V] = rv_all

    row = lax.broadcasted_iota(jnp.int32, (BLOCK, BLOCK), 0)
    col = lax.broadcasted_iota(jnp.int32, (BLOCK, BLOCK), 1)
    tri_f = jnp.where(col <= row, 1.0, 0.0).astype(BF16)
    tri_b = jnp.where(col >= row, 1.0, 0.0).astype(BF16)
    bf = sum(_dot(tri_f, p) for p in la_f)
    cb = sum(_dot(tri_b, p) for p in la_b)
    cg_ref[:, :GLA_QK] = bf
    cg_ref[:, GLA_QK:] = cb

    cos = cos_ref[...]
    sin = sin_ref[...]
    ur_ref[...] = sr_ref[...]
    for hh in range(RET_HEADS):
        lo = hh * RET_DK
        q = rq_all[:, lo:lo + RET_DK]
        k = rk_all[:, lo:lo + RET_DK]
        q = q * cos + pltpu.roll(q, RET_DK // 2, axis=1) * sin
        k = k * cos + pltpu.roll(k, RET_DK // 2, axis=1) * sin
        proj_ref[:, RQ + lo:RQ + lo + RET_DK] = q.astype(BF16)
        proj_ref[:, RK + lo:RK + lo + RET_DK] = k.astype(BF16)
        v = rv_all[:, lo:lo + RET_DV]
        kd = (k * eb_ref[:, lo:lo + RET_DK]).astype(BF16)
        sr_ref[hh] = sr_ref[hh] * ret_block_decay[hh] + _dot(kd, v, TN)

    ug_ref[...] = sg_ref[...]
    ctot = cb[0:1, :]
    kb = (gk * jnp.exp(ctot - cb)).astype(BF16)
    sg_ref[...] = sg_ref[...] * _as_column(jnp.exp(ctot)) + _gla_state_increment(kb, gv)


def _mix_kernel(xprev_ref, proj_ref, cg_ref, ug_ref, ur_ref, dmat_ref, df_ref, db_ref, ef_ref,
                gnw_ref, rnw_ref, rnb_ref, wout_ref, mnw_ref, w1_ref, w2_ref, fnw_ref,
                y_ref, fg_ref, fr_ref, mixed_ref, *, blocks_per_seq, ret_block_decay, final_norm):
    g = pl.program_id(0)

    @pl.when(g % blocks_per_seq == 0)
    def _():
        fg_ref[...] = jnp.zeros_like(fg_ref)
        fr_ref[...] = jnp.zeros_like(fr_ref)

    @pl.when(g == 0)
    def _():
        mixed_ref[...] = jnp.zeros_like(mixed_ref)

    dense = {}

    def dense_head():
        x1 = xprev_ref[...] + _dot(mixed_ref[...], wout_ref[...])
        dense["h2"] = _rms(x1, mnw_ref[...]).astype(BF16)
        dense["acc"] = x1

    def dense_tile(f):
        ff = jnp.maximum(_dot(dense["h2"], w1_ref[:, f * FF_TILE:(f + 1) * FF_TILE]), 0.0)
        dense["acc"] = dense["acc"] + _dot((ff * ff).astype(BF16),
                                           w2_ref[f * FF_TILE:(f + 1) * FF_TILE, :])

    def dense_finish():
        y_ref[...] = _rms(dense["acc"], fnw_ref[...]) if final_norm else dense["acc"]

    n_ff = D_FF // FF_TILE
    dense_head()

    def gla_operands(a, forward):
        r0 = a * CHUNK
        if forward:
            cum = cg_ref[r0:r0 + CHUNK, 0:GLA_QK]
            if a > 0:
                cum = cum - cg_ref[r0 - 1:r0, 0:GLA_QK]
        else:
            cum = cg_ref[r0:r0 + CHUNK, GLA_QK:2 * GLA_QK]
            if a < N_CHUNKS - 1:
                cum = cum - cg_ref[r0 + CHUNK:r0 + CHUNK + 1, GLA_QK:2 * GLA_QK]
        e = jnp.exp(cum)
        e_edge = e[CHUNK - 1:CHUNK, :] if forward else e[0:1, :]
        q = proj_ref[r0:r0 + CHUNK, GQ:GQ + GLA_QK].astype(F32)
        k = proj_ref[r0:r0 + CHUNK, GK:GK + GLA_QK].astype(F32)
        ki = k * jnp.exp(-cum)
        return dict(qd=(q * e).astype(BF16), ki=ki.astype(BF16), ke=(ki * e_edge).astype(BF16),
                    e_edge=e_edge, v=proj_ref[r0:r0 + CHUNK, GV:GV + GLA_V])

    sweeps = {True: list(range(N_CHUNKS)), False: list(reversed(range(N_CHUNKS)))}
    ops = {(fw, a): gla_operands(a, fw) for fw in (True, False) for a in sweeps[fw]}

    ret = []
    for h in range(RET_HEADS):
        lo = h * RET_DK
        q = proj_ref[:, RQ + lo:RQ + lo + RET_DK]
        k = proj_ref[:, RK + lo:RK + lo + RET_DK]
        qf = q.astype(F32)
        ret.append(dict(
            q=q, k=k, v=proj_ref[:, RV + lo:RV + lo + RET_DV],
            q_fwd=(qf * df_ref[:, lo:lo + RET_DK]).astype(BF16),
            q_bwd=(qf * db_ref[:, lo:lo + RET_DK]).astype(BF16),
            kd=(k.astype(F32) * ef_ref[:, lo:lo + RET_DK]).astype(BF16)))

    n_pairs = GLA_HEADS // 2
    lane = lax.broadcasted_iota(jnp.int32, (CHUNK, LANES), 1)
    even_lanes = lane < GLA_DK
    for key, op in ops.items():
        fw, a = key
        if fw or a > 0:
            op["inc"] = _gla_state_increment(op["ke"], op["v"])
        op["s"] = []
        for pair in range(n_pairs):
            ki = op["ki"][:, pair * LANES:(pair + 1) * LANES]
            keys = jnp.concatenate([jnp.where(even_lanes, 0.0, ki), jnp.where(even_lanes, ki, 0.0)],
                                   axis=0).astype(BF16)
            op["s"].append(_dot(op["qd"][:, pair * LANES:(pair + 1) * LANES], keys, NT))
    for h, r in enumerate(ret):
        r["s"] = _dot(r["q"], r["k"], NT)
        states = jnp.concatenate([fr_ref[h], ur_ref[h]], axis=0).astype(BF16)
        r["inter"] = _dot(jnp.concatenate([r["q_fwd"], r["q_bwd"]], axis=1), states)
        fr_ref[h] = fr_ref[h] * ret_block_decay[h] + _dot(r["kd"], r["v"], TN)

    for f in range(n_ff // 2):
        dense_tile(f)

    ri = lax.broadcasted_iota(jnp.int32, (CHUNK, LANES), 0)
    ci = lane % CHUNK
    for fw in (True, False):
        state = fg_ref[...] if fw else ug_ref[...]
        keep = (ci <= ri) if fw else (ci > ri)
        for a in sweeps[fw]:
            op = ops[(fw, a)]
            op["state"] = state.astype(BF16)
            op["p"] = [jnp.where(keep, s, 0.0).astype(BF16) for s in op["s"]]
            if "inc" in op:
                state = state * _as_column(op["e_edge"]) + op["inc"]
        if fw:
            fg_ref[...] = state
    for h, r in enumerate(ret):
        r["p"] = (r["s"] * dmat_ref[h]).astype(BF16)

    for op in ops.values():
        outs = []
        for h in range(GLA_HEADS):
            pair, odd = divmod(h, 2)
            qd = op["qd"][:, pair * LANES:(pair + 1) * LANES]
            p = op["p"][pair]
            v = op["v"][:, h * GLA_DV:(h + 1) * GLA_DV]
            s_h = op["state"][h * GLA_DK:(h + 1) * GLA_DK, :]
            if odd:
                lhs = jnp.where(even_lanes, p, qd)
                rhs = jnp.concatenate([v, s_h], axis=0)
            else:
                lhs = jnp.where(even_lanes, qd, p)
                rhs = jnp.concatenate([s_h, v], axis=0)
            outs.append(_dot(lhs, rhs))
        op["o"] = jnp.concatenate(outs, axis=1)
    for r in ret:
        r["o"] = _dot(r["p"], r["v"]) + r["inter"]

    for f in range(n_ff // 2, n_ff):
        dense_tile(f)

    o_gla = jnp.concatenate([ops[(True, a)]["o"] + ops[(False, a)]["o"] for a in range(N_CHUNKS)],
                            axis=0)
    mixed = []
    gnw = gnw_ref[...]
    for h in range(GLA_HEADS):
        lo = h * GLA_DV
        oh = _rms(o_gla[:, lo:lo + GLA_DV], gnw)
        mixed.append(oh * _silu(proj_ref[:, GG + lo:GG + lo + GLA_DV].astype(F32)))
    for h, r in enumerate(ret):
        lo = h * RET_DV
        o = r["o"]
        mu = jnp.mean(o, axis=-1, keepdims=True)
        oc = o - mu
        var = jnp.mean(oc * oc, axis=-1, keepdims=True)
        on = oc * lax.rsqrt(var + EPS) * rnw_ref[:, lo:lo + RET_DV] + rnb_ref[:, lo:lo + RET_DV]
        mixed.append(on * _silu(proj_ref[:, RG + lo:RG + lo + RET_DV].astype(F32)))
    mixed_ref[...] = jnp.concatenate(mixed, axis=1).astype(BF16)
    dense_finish()


def _const_spec(shape):
    zeros = (0,) * len(shape)
    return pl.BlockSpec(shape, lambda *_: zeros, pipeline_mode=pl.Buffered(1))


def _retention_tables():
    log_gamma = np.log1p(-np.power(2.0, -5.0 - np.arange(RET_HEADS, dtype=np.float64)))
    i = np.arange(BLOCK, dtype=np.float64)
    scale = RET_DK ** -0.5
    dist = np.abs(i[:, None] - i[None, :])
    dmat = np.exp(log_gamma[:, None, None] * dist[None]) * scale

    def expand(t):
        return np.repeat(t, RET_DK, axis=1).astype(np.float32)

    df = expand(np.exp(np.outer(i + 1.0, log_gamma)) * scale)
    db = expand(np.exp(np.outer(BLOCK - i, log_gamma)) * scale)
    ef = expand(np.exp(np.outer(BLOCK - 1.0 - i, log_gamma)))
    eb = expand(np.exp(np.outer(i, log_gamma)))
    block_decay = tuple(float(g) for g in np.exp(BLOCK * log_gamma))
    return dmat.astype(np.float32), df, db, ef, eb, block_decay


def _rotary_tables(seq_len):
    half = RET_DK // 2
    inv_freq = jnp.power(ROPE_BASE, -jnp.arange(half, dtype=F32) / half)
    ang = jnp.arange(seq_len, dtype=F32)[:, None] * inv_freq[None, :]
    cos, sin = jnp.cos(ang), jnp.sin(ang)
    return jnp.concatenate([cos, cos], axis=1), jnp.concatenate([-sin, sin], axis=1)


def _prepare_layer(attn_norm_w, w_in, w_alpha_fwd, b_alpha_fwd, w_alpha_bwd, b_alpha_bwd,
                   gla_norm_w, ret_norm_w, ret_norm_b, w_out, mlp_norm_w, w_ff1, w_ff2):
    o = np.cumsum((0, GLA_QK, GLA_QK, GLA_V, GLA_V, GLA_LOWRANK, GLA_LOWRANK,
                   RET_QK, RET_QK, RET_V, RET_V))
    gq, gk, gv, gg, ga_f, ga_b, rq, rk, rv, rg = (w_in[:, o[i]:o[i + 1]] for i in range(10))
    pad = jnp.zeros((D_MODEL, LANES - 2 * GLA_LOWRANK), w_in.dtype)
    w_in_p = jnp.concatenate([gq * (GLA_DK ** -0.5), gk, gv, gg, rq, rk, rv, rg, ga_f, ga_b, pad],
                             axis=1).astype(BF16)
    w_al = jnp.zeros((LANES, 2 * GLA_QK), F32)
    w_al = w_al.at[:GLA_LOWRANK, :GLA_QK].set(w_alpha_fwd)
    w_al = w_al.at[GLA_LOWRANK:2 * GLA_LOWRANK, GLA_QK:].set(w_alpha_bwd)
    b_al = jnp.concatenate([b_alpha_fwd, b_alpha_bwd])[None, :].astype(F32)
    return dict(
        anw=attn_norm_w[None, :].astype(F32), w_in=w_in_p, w_al=w_al.astype(BF16), b_al=b_al,
        gnw=gla_norm_w[None, :].astype(F32), rnw=ret_norm_w[None, :].astype(F32),
        rnb=ret_norm_b[None, :].astype(F32), w_out=w_out.astype(BF16),
        mnw=mlp_norm_w[None, :].astype(F32), w1=w_ff1.astype(BF16), w2=w_ff2.astype(BF16))


def _layer(x, lw, fnw, final_norm, interpret=False):
    batch, seq_len, _ = x.shape
    assert seq_len % BLOCK == 0
    nb = seq_len // BLOCK
    dmat, df, db, ef, eb, block_decay = _retention_tables()
    cos, sin = _rotary_tables(seq_len)
    params = pltpu.CompilerParams(dimension_semantics=("arbitrary", "arbitrary"),
                                  vmem_limit_bytes=VMEM_LIMIT_BYTES)

    def rev(b, n):
        return (b, nb - 1 - n, 0)

    proj, cg, ug, ur = pl.pallas_call(
        functools.partial(_proj_kernel, ret_block_decay=block_decay),
        grid=(batch, nb),
        in_specs=[
            pl.BlockSpec((None, BLOCK, D_MODEL), rev),
            pl.BlockSpec((BLOCK, LANES), lambda b, n: (nb - 1 - n, 0)),
            pl.BlockSpec((BLOCK, LANES), lambda b, n: (nb - 1 - n, 0)),
            _const_spec((1, D_MODEL)),
            _const_spec((D_MODEL, W_IN_COLS)),
            _const_spec((LANES, 2 * GLA_QK)),
            _const_spec((1, 2 * GLA_QK)),
            _const_spec((BLOCK, RET_QK)),
        ],
        out_specs=[
            pl.BlockSpec((None, BLOCK, P_COLS), rev),
            pl.BlockSpec((None, BLOCK, 2 * GLA_QK), rev),
            pl.BlockSpec((None, None, GLA_QK, GLA_DV), lambda b, n: (b, nb - 1 - n, 0, 0)),
            pl.BlockSpec((None, None, RET_HEADS, RET_DK, RET_DV), lambda b, n: (b, nb - 1 - n, 0, 0, 0)),
        ],
        out_shape=[
            jax.ShapeDtypeStruct((batch, seq_len, P_COLS), BF16),
            jax.ShapeDtypeStruct((batch, seq_len, 2 * GLA_QK), F32),
            jax.ShapeDtypeStruct((batch, nb, GLA_QK, GLA_DV), F32),
            jax.ShapeDtypeStruct((batch, nb, RET_HEADS, RET_DK, RET_DV), F32),
        ],
        scratch_shapes=[pltpu.VMEM((GLA_QK, GLA_DV), F32),
                        pltpu.VMEM((RET_HEADS, RET_DK, RET_DV), F32)],
        compiler_params=params,
        name="proj_bwd_state",
        interpret=interpret,
    )(x, cos, sin, lw["anw"], lw["w_in"], lw["w_al"], lw["b_al"], eb)

    total = batch * nb

    def cur(g):
        blk = jnp.minimum(g, total - 1)
        return blk // nb, blk % nb

    def prev(g):
        blk = jnp.maximum(g - 1, 0)
        return blk // nb, blk % nb

    return pl.pallas_call(
        functools.partial(_mix_kernel, blocks_per_seq=nb, ret_block_decay=block_decay,
                          final_norm=final_norm),
        grid=(total + 1,),
        in_specs=[
            pl.BlockSpec((None, BLOCK, D_MODEL), lambda g: (*prev(g), 0)),
            pl.BlockSpec((None, BLOCK, P_COLS), lambda g: (*cur(g), 0)),
            pl.BlockSpec((None, BLOCK, 2 * GLA_QK), lambda g: (*cur(g), 0)),
            pl.BlockSpec((None, None, GLA_QK, GLA_DV), lambda g: (*cur(g), 0, 0)),
            pl.BlockSpec((None, None, RET_HEADS, RET_DK, RET_DV), lambda g: (*cur(g), 0, 0, 0)),
            _const_spec((RET_HEADS, BLOCK, BLOCK)),
            _const_spec((BLOCK, RET_QK)),
            _const_spec((BLOCK, RET_QK)),
            _const_spec((BLOCK, RET_QK)),
            _const_spec((1, GLA_DV)),
            _const_spec((1, RET_V)),
            _const_spec((1, RET_V)),
            _const_spec((D_MODEL, D_MODEL)),
            _const_spec((1, D_MODEL)),
            _const_spec((D_MODEL, D_FF)),
            _const_spec((D_FF, D_MODEL)),
            _const_spec((1, D_MODEL)),
        ],
        out_specs=pl.BlockSpec((None, BLOCK, D_MODEL), lambda g: (*prev(g), 0)),
        out_shape=jax.ShapeDtypeStruct((batch, seq_len, D_MODEL), F32),
        scratch_shapes=[pltpu.VMEM((GLA_QK, GLA_DV), F32),
                        pltpu.VMEM((RET_HEADS, RET_DK, RET_DV), F32),
                        pltpu.VMEM((BLOCK, D_MODEL), BF16)],
        compiler_params=pltpu.CompilerParams(dimension_semantics=("arbitrary",),
                                             vmem_limit_bytes=VMEM_LIMIT_BYTES),
        name="mix_mlp",
        interpret=interpret,
    )(x, proj, cg, ug, ur, dmat, df, db, ef, lw["gnw"], lw["rnw"], lw["rnb"], lw["w_out"],
      lw["mnw"], lw["w1"], lw["w2"], fnw)


def _trunk(x, layers, fnw, interpret=False):
    for i, lw in enumerate(layers):
        x = _layer(x, lw, fnw, final_norm=(i == len(layers) - 1), interpret=interpret)
    return x


def kernel(x_prompt, x_sample, attn_norm_w, w_in, w_alpha_fwd, b_alpha_fwd, w_alpha_bwd, b_alpha_bwd,
           gla_norm_w, ret_norm_w, ret_norm_b, w_out, mlp_norm_w, w_ff1, w_ff2, final_norm_w):
    stacked = (attn_norm_w, w_in, w_alpha_fwd, b_alpha_fwd, w_alpha_bwd, b_alpha_bwd,
               gla_norm_w, ret_norm_w, ret_norm_b, w_out, mlp_norm_w, w_ff1, w_ff2)
    layers = [_prepare_layer(*(w[l] for w in stacked)) for l in range(w_in.shape[0])]
    fnw = final_norm_w[None, :].astype(F32)
    return _trunk(x_prompt, layers, fnw), _trunk(x_sample, layers, fnw)
```

```python
import functools

import numpy as np
import jax
import jax.numpy as jnp
from jax import lax
from jax.experimental import pallas as pl
from jax.experimental.pallas import tpu as pltpu

F32 = jnp.float32
BF16 = jnp.bfloat16

D_MODEL = 1024
GLA_HEADS = 4
GLA_DK = 64
GLA_DV = 128
GLA_QK = GLA_HEADS * GLA_DK
GLA_V = GLA_HEADS * GLA_DV
GLA_LOWRANK = 16
GLA_NORMALIZER = 16.0
LOG_GATE_MIN = -1.0
RET_HEADS = 4
RET_DK = 128
RET_DV = 128
RET_QK = RET_HEADS * RET_DK
RET_V = RET_HEADS * RET_DV
D_FF = 4 * D_MODEL
ROPE_BASE = 10000.0
EPS = 1e-6

LANES = 128
BLOCK = 256
SUBS = 2
STEP = SUBS * BLOCK
CHUNK = 64
N_CHUNKS = BLOCK // CHUNK
FF_TILE = 512
VMEM_LIMIT_BYTES = 56 * 1024 * 1024

GQ, GK, GV, GG = 0, 256, 512, 1024
RQ, RK, RV, RG = 1536, 2048, 2560, 3072
P_COLS = 3584
GA = P_COLS
W_IN_COLS = P_COLS + LANES

NT = (((1,), (1,)), ((), ()))
TN = (((0,), (0,)), ((), ()))


def _dot(a, b, dims=None):
    if dims is None:
        return jnp.dot(a, b, preferred_element_type=F32)
    return lax.dot_general(a, b, dims, preferred_element_type=F32)


def _rms(x, w):
    return x * lax.rsqrt(jnp.mean(x * x, axis=-1, keepdims=True) + EPS) * w


def _silu(g):
    return g * jax.nn.sigmoid(g)


def _split3(a):
    h1 = a.astype(BF16)
    r1 = a - h1.astype(F32)
    h2 = r1.astype(BF16)
    r2 = r1 - h2.astype(F32)
    return h1, h2, r2.astype(BF16)


def _gla_state_increment(ke, v):
    blocks = []
    for pair in range(GLA_HEADS // 2):
        full = _dot(ke[:, pair * LANES:(pair + 1) * LANES],
                    v[:, pair * 2 * GLA_DV:(pair + 1) * 2 * GLA_DV], TN)
        blocks += [full[:GLA_DK, :GLA_DV], full[GLA_DK:, GLA_DV:]]
    return jnp.concatenate(blocks, axis=0)


def _as_column(row):
    return jnp.broadcast_to(row, (LANES, GLA_QK)).T


def _proj_kernel(x_ref, cos_ref, sin_ref, anw_ref, win_ref, wal_ref, bal_ref, eb_ref,
                 proj_ref, cg_ref, ug_ref, ur_ref, sg_ref, sr_ref, *, ret_block_decay):
    @pl.when(pl.program_id(1) == 0)
    def _():
        sg_ref[...] = jnp.zeros_like(sg_ref)
        sr_ref[...] = jnp.zeros_like(sr_ref)

    row = lax.broadcasted_iota(jnp.int32, (BLOCK, BLOCK), 0)
    col = lax.broadcasted_iota(jnp.int32, (BLOCK, BLOCK), 1)
    tri_f = jnp.where(col <= row, 1.0, 0.0).astype(BF16)
    tri_b = jnp.where(col >= row, 1.0, 0.0).astype(BF16)

    def block(sub):
        rows = slice(sub * BLOCK, (sub + 1) * BLOCK)
        h = _rms(x_ref[rows, :], anw_ref[...]).astype(BF16)

        def proj(lo, width):
            return _dot(h, win_ref[:, lo:lo + width])

        rg_ga = proj(RG, RET_V + LANES)
        proj_ref[rows, RG:RG + RET_V] = rg_ga[:, :RET_V].astype(BF16)
        ga = rg_ga[:, RET_V:].astype(BF16)
        z = _dot(ga, wal_ref[...]) + bal_ref[...]
        la = jnp.maximum(jax.nn.log_sigmoid(z) * (1.0 / GLA_NORMALIZER), LOG_GATE_MIN)
        la_f = _split3(la[:, :GLA_QK])
        la_b = _split3(la[:, GLA_QK:])

        gqk = proj(GQ, 2 * GLA_QK)
        proj_ref[rows, GQ:GQ + 2 * GLA_QK] = gqk.astype(BF16)
        gk = gqk[:, GLA_QK:]
        gv = proj(GV, GLA_V).astype(BF16)
        proj_ref[rows, GV:GV + GLA_V] = gv
        proj_ref[rows, GG:GG + GLA_V] = proj(GG, GLA_V).astype(BF16)
        rq_all = proj(RQ, RET_QK)
        rk_all = proj(RK, RET_QK)
        rv_all = proj(RV, RET_V).astype(BF16)
        proj_ref[rows, RV:RV + RET_V] = rv_all

        bf = sum(_dot(tri_f, p) for p in la_f)
        cb = sum(_dot(tri_b, p) for p in la_b)
        cg_ref[rows, :GLA_QK] = bf
        cg_ref[rows, GLA_QK:] = cb

        cos = cos_ref[rows, :]
        sin = sin_ref[rows, :]
        ur_ref[sub] = sr_ref[...]
        for hh in range(RET_HEADS):
            lo = hh * RET_DK
            q = rq_all[:, lo:lo + RET_DK]
            k = rk_all[:, lo:lo + RET_DK]
            q = q * cos + pltpu.roll(q, RET_DK // 2, axis=1) * sin
            k = k * cos + pltpu.roll(k, RET_DK // 2, axis=1) * sin
            proj_ref[rows, RQ + lo:RQ + lo + RET_DK] = q.astype(BF16)
            proj_ref[rows, RK + lo:RK + lo + RET_DK] = k.astype(BF16)
            v = rv_all[:, lo:lo + RET_DV]
            kd = (k * eb_ref[:, lo:lo + RET_DK]).astype(BF16)
            sr_ref[hh] = sr_ref[hh] * ret_block_decay[hh] + _dot(kd, v, TN)

        ug_ref[sub] = sg_ref[...]
        ctot = cb[0:1, :]
        kb = (gk * jnp.exp(ctot - cb)).astype(BF16)
        sg_ref[...] = sg_ref[...] * _as_column(jnp.exp(ctot)) + _gla_state_increment(kb, gv)

    for sub in reversed(range(SUBS)):
        block(sub)


def _mix_kernel(xprev_ref, proj_ref, cg_ref, ug_ref, ur_ref, dmat_ref, df_ref, db_ref, ef_ref,
                gnw_ref, rnw_ref, rnb_ref, wout_ref, mnw_ref, w1_ref, w2_ref, fnw_ref,
                y_ref, fg_ref, fr_ref, mixed_ref, *, steps_per_seq, ret_block_decay, final_norm):
    g = pl.program_id(0)

    @pl.when(g % steps_per_seq == 0)
    def _():
        fg_ref[...] = jnp.zeros_like(fg_ref)
        fr_ref[...] = jnp.zeros_like(fr_ref)

    @pl.when(g == 0)
    def _():
        mixed_ref[...] = jnp.zeros_like(mixed_ref)

    dense = {}

    def dense_head():
        x1 = xprev_ref[...] + _dot(mixed_ref[...], wout_ref[...])
        dense["h2"] = _rms(x1, mnw_ref[...]).astype(BF16)
        dense["acc"] = x1

    def dense_tile(f):
        ff = jnp.maximum(_dot(dense["h2"], w1_ref[:, f * FF_TILE:(f + 1) * FF_TILE]), 0.0)
        dense["acc"] = dense["acc"] + _dot((ff * ff).astype(BF16),
                                           w2_ref[f * FF_TILE:(f + 1) * FF_TILE, :])

    def dense_finish():
        y_ref[...] = _rms(dense["acc"], fnw_ref[...]) if final_norm else dense["acc"]

    n_ff = D_FF // FF_TILE
    dense_head()

    def gla_operands(sub, a, forward):
        r0 = sub * BLOCK + a * CHUNK
        if forward:
            cum = cg_ref[r0:r0 + CHUNK, 0:GLA_QK]
            if a > 0:
                cum = cum - cg_ref[r0 - 1:r0, 0:GLA_QK]
        else:
            cum = cg_ref[r0:r0 + CHUNK, GLA_QK:2 * GLA_QK]
            if a < N_CHUNKS - 1:
                cum = cum - cg_ref[r0 + CHUNK:r0 + CHUNK + 1, GLA_QK:2 * GLA_QK]
        e = jnp.exp(cum)
        e_edge = e[CHUNK - 1:CHUNK, :] if forward else e[0:1, :]
        q = proj_ref[r0:r0 + CHUNK, GQ:GQ + GLA_QK].astype(F32)
        k = proj_ref[r0:r0 + CHUNK, GK:GK + GLA_QK].astype(F32)
        ki = k * jnp.exp(-cum)
        return dict(qd=(q * e).astype(BF16), ki=ki.astype(BF16), ke=(ki * e_edge).astype(BF16),
                    e_edge=e_edge, v=proj_ref[r0:r0 + CHUNK, GV:GV + GLA_V])

    sweeps = {True: list(range(N_CHUNKS)), False: list(reversed(range(N_CHUNKS)))}
    ops = {(sub, fw, a): gla_operands(sub, a, fw)
           for sub in range(SUBS) for fw in (True, False) for a in sweeps[fw]}

    ret = {}
    for sub in range(SUBS):
        rows = slice(sub * BLOCK, (sub + 1) * BLOCK)
        for h in range(RET_HEADS):
            lo = h * RET_DK
            q = proj_ref[rows, RQ + lo:RQ + lo + RET_DK]
            k = proj_ref[rows, RK + lo:RK + lo + RET_DK]
            qf = q.astype(F32)
            ret[(sub, h)] = dict(
                q=q, k=k, v=proj_ref[rows, RV + lo:RV + lo + RET_DV],
                q_both=jnp.concatenate([(qf * df_ref[:, lo:lo + RET_DK]).astype(BF16),
                                        (qf * db_ref[:, lo:lo + RET_DK]).astype(BF16)], axis=1),
                kd=(k.astype(F32) * ef_ref[:, lo:lo + RET_DK]).astype(BF16))

    n_pairs = GLA_HEADS // 2
    lane = lax.broadcasted_iota(jnp.int32, (CHUNK, LANES), 1)
    even_lanes = lane < GLA_DK
    for key, op in ops.items():
        _, fw, a = key
        if fw or a > 0:
            op["inc"] = _gla_state_increment(op["ke"], op["v"])
        op["s"] = []
        for pair in range(n_pairs):
            ki = op["ki"][:, pair * LANES:(pair + 1) * LANES]
            keys = jnp.concatenate([jnp.where(even_lanes, 0.0, ki), jnp.where(even_lanes, ki, 0.0)],
                                   axis=0).astype(BF16)
            op["s"].append(_dot(op["qd"][:, pair * LANES:(pair + 1) * LANES], keys, NT))
    for r in ret.values():
        r["inc"] = _dot(r["kd"], r["v"], TN)
        r["s"] = _dot(r["q"], r["k"], NT)
    for h in range(RET_HEADS):
        state = fr_ref[h]
        for sub in range(SUBS):
            r = ret[(sub, h)]
            states = jnp.concatenate([state, ur_ref[sub, h]], axis=0).astype(BF16)
            r["inter"] = _dot(r["q_both"], states)
            state = state * ret_block_decay[h] + r["inc"]
        fr_ref[h] = state

    for f in range(n_ff // 2):
        dense_tile(f)

    ri = lax.broadcasted_iota(jnp.int32, (CHUNK, LANES), 0)
    ci = lane % CHUNK
    state = fg_ref[...]
    for sub in range(SUBS):
        for fw in (True, False):
            if not fw:
                fwd_state, state = state, ug_ref[sub]
            keep = (ci <= ri) if fw else (ci > ri)
            for a in sweeps[fw]:
                op = ops[(sub, fw, a)]
                op["state"] = state.astype(BF16)
                op["p"] = [jnp.where(keep, s, 0.0).astype(BF16) for s in op["s"]]
                if "inc" in op:
                    state = state * _as_column(op["e_edge"]) + op["inc"]
        state = fwd_state
    fg_ref[...] = state
    for (sub, h), r in ret.items():
        r["p"] = (r["s"] * dmat_ref[h]).astype(BF16)

    for op in ops.values():
        outs = []
        for h in range(GLA_HEADS):
            pair, odd = divmod(h, 2)
            qd = op["qd"][:, pair * LANES:(pair + 1) * LANES]
            p = op["p"][pair]
            v = op["v"][:, h * GLA_DV:(h + 1) * GLA_DV]
            s_h = op["state"][h * GLA_DK:(h + 1) * GLA_DK, :]
            if odd:
                lhs = jnp.where(even_lanes, p, qd)
                rhs = jnp.concatenate([v, s_h], axis=0)
            else:
                lhs = jnp.where(even_lanes, qd, p)
                rhs = jnp.concatenate([s_h, v], axis=0)
            outs.append(_dot(lhs, rhs))
        op["o"] = jnp.concatenate(outs, axis=1)
    for r in ret.values():
        r["o"] = _dot(r["p"], r["v"]) + r["inter"]

    for f in range(n_ff // 2, n_ff):
        dense_tile(f)

    gnw = gnw_ref[...]
    for sub in range(SUBS):
        rows = slice(sub * BLOCK, (sub + 1) * BLOCK)
        o_gla = jnp.concatenate([ops[(sub, True, a)]["o"] + ops[(sub, False, a)]["o"]
                                 for a in range(N_CHUNKS)], axis=0)
        mixed = []
        for h in range(GLA_HEADS):
            lo = h * GLA_DV
            oh = _rms(o_gla[:, lo:lo + GLA_DV], gnw)
            mixed.append(oh * _silu(proj_ref[rows, GG + lo:GG + lo + GLA_DV].astype(F32)))
        for h in range(RET_HEADS):
            lo = h * RET_DV
            o = ret[(sub, h)]["o"]
            mu = jnp.mean(o, axis=-1, keepdims=True)
            oc = o - mu
            var = jnp.mean(oc * oc, axis=-1, keepdims=True)
            on = oc * lax.rsqrt(var + EPS) * rnw_ref[:, lo:lo + RET_DV] + rnb_ref[:, lo:lo + RET_DV]
            mixed.append(on * _silu(proj_ref[rows, RG + lo:RG + lo + RET_DV].astype(F32)))
        mixed_ref[rows, :] = jnp.concatenate(mixed, axis=1).astype(BF16)
    dense_finish()


def _const_spec(shape):
    zeros = (0,) * len(shape)
    return pl.BlockSpec(shape, lambda *_: zeros, pipeline_mode=pl.Buffered(1))


def _retention_tables():
    log_gamma = np.log1p(-np.power(2.0, -5.0 - np.arange(RET_HEADS, dtype=np.float64)))
    i = np.arange(BLOCK, dtype=np.float64)
    scale = RET_DK ** -0.5
    dist = np.abs(i[:, None] - i[None, :])
    dmat = np.exp(log_gamma[:, None, None] * dist[None]) * scale

    def expand(t):
        return np.repeat(t, RET_DK, axis=1).astype(np.float32)

    df = expand(np.exp(np.outer(i + 1.0, log_gamma)) * scale)
    db = expand(np.exp(np.outer(BLOCK - i, log_gamma)) * scale)
    ef = expand(np.exp(np.outer(BLOCK - 1.0 - i, log_gamma)))
    eb = expand(np.exp(np.outer(i, log_gamma)))
    block_decay = tuple(float(g) for g in np.exp(BLOCK * log_gamma))
    return dmat.astype(np.float32), df, db, ef, eb, block_decay


def _rotary_tables(seq_len):
    half = RET_DK // 2
    inv_freq = jnp.power(ROPE_BASE, -jnp.arange(half, dtype=F32) / half)
    ang = jnp.arange(seq_len, dtype=F32)[:, None] * inv_freq[None, :]
    cos, sin = jnp.cos(ang), jnp.sin(ang)
    return jnp.concatenate([cos, cos], axis=1), jnp.concatenate([-sin, sin], axis=1)


def _prepare_layer(attn_norm_w, w_in, w_alpha_fwd, b_alpha_fwd, w_alpha_bwd, b_alpha_bwd,
                   gla_norm_w, ret_norm_w, ret_norm_b, w_out, mlp_norm_w, w_ff1, w_ff2):
    o = np.cumsum((0, GLA_QK, GLA_QK, GLA_V, GLA_V, GLA_LOWRANK, GLA_LOWRANK,
                   RET_QK, RET_QK, RET_V, RET_V))
    gq, gk, gv, gg, ga_f, ga_b, rq, rk, rv, rg = (w_in[:, o[i]:o[i + 1]] for i in range(10))
    pad = jnp.zeros((D_MODEL, LANES - 2 * GLA_LOWRANK), w_in.dtype)
    w_in_p = jnp.concatenate([gq * (GLA_DK ** -0.5), gk, gv, gg, rq, rk, rv, rg, ga_f, ga_b, pad],
                             axis=1).astype(BF16)
    w_al = jnp.zeros((LANES, 2 * GLA_QK), F32)
    w_al = w_al.at[:GLA_LOWRANK, :GLA_QK].set(w_alpha_fwd)
    w_al = w_al.at[GLA_LOWRANK:2 * GLA_LOWRANK, GLA_QK:].set(w_alpha_bwd)
    b_al = jnp.concatenate([b_alpha_fwd, b_alpha_bwd])[None, :].astype(F32)
    return dict(
        anw=attn_norm_w[None, :].astype(F32), w_in=w_in_p, w_al=w_al.astype(BF16), b_al=b_al,
        gnw=gla_norm_w[None, :].astype(F32), rnw=ret_norm_w[None, :].astype(F32),
        rnb=ret_norm_b[None, :].astype(F32), w_out=w_out.astype(BF16),
        mnw=mlp_norm_w[None, :].astype(F32), w1=w_ff1.astype(BF16), w2=w_ff2.astype(BF16))


def _layer(x, lw, fnw, final_norm, interpret=False):
    batch, seq_len, _ = x.shape
    assert seq_len % STEP == 0
    ns = seq_len // STEP
    nb = seq_len // BLOCK
    dmat, df, db, ef, eb, block_decay = _retention_tables()
    cos, sin = _rotary_tables(seq_len)

    def rev(b, n):
        return (b, ns - 1 - n, 0)

    proj, cg, ug, ur = pl.pallas_call(
        functools.partial(_proj_kernel, ret_block_decay=block_decay),
        grid=(batch, ns),
        in_specs=[
            pl.BlockSpec((None, STEP, D_MODEL), rev),
            pl.BlockSpec((STEP, LANES), lambda b, n: (ns - 1 - n, 0)),
            pl.BlockSpec((STEP, LANES), lambda b, n: (ns - 1 - n, 0)),
            _const_spec((1, D_MODEL)),
            _const_spec((D_MODEL, W_IN_COLS)),
            _const_spec((LANES, 2 * GLA_QK)),
            _const_spec((1, 2 * GLA_QK)),
            _const_spec((BLOCK, RET_QK)),
        ],
        out_specs=[
            pl.BlockSpec((None, STEP, P_COLS), rev),
            pl.BlockSpec((None, STEP, 2 * GLA_QK), rev),
            pl.BlockSpec((None, SUBS, GLA_QK, GLA_DV), lambda b, n: (b, ns - 1 - n, 0, 0)),
            pl.BlockSpec((None, SUBS, RET_HEADS, RET_DK, RET_DV),
                         lambda b, n: (b, ns - 1 - n, 0, 0, 0)),
        ],
        out_shape=[
            jax.ShapeDtypeStruct((batch, seq_len, P_COLS), BF16),
            jax.ShapeDtypeStruct((batch, seq_len, 2 * GLA_QK), F32),
            jax.ShapeDtypeStruct((batch, nb, GLA_QK, GLA_DV), F32),
            jax.ShapeDtypeStruct((batch, nb, RET_HEADS, RET_DK, RET_DV), F32),
        ],
        scratch_shapes=[pltpu.VMEM((GLA_QK, GLA_DV), F32),
                        pltpu.VMEM((RET_HEADS, RET_DK, RET_DV), F32)],
        compiler_params=pltpu.CompilerParams(dimension_semantics=("arbitrary", "arbitrary"),
                                             vmem_limit_bytes=VMEM_LIMIT_BYTES),
        name="proj_bwd_state",
        interpret=interpret,
    )(x, cos, sin, lw["anw"], lw["w_in"], lw["w_al"], lw["b_al"], eb)

    total = batch * ns

    def cur(g):
        s = jnp.minimum(g, total - 1)
        return s // ns, s % ns

    def prev(g):
        s = jnp.maximum(g - 1, 0)
        return s // ns, s % ns

    return pl.pallas_call(
        functools.partial(_mix_kernel, steps_per_seq=ns, ret_block_decay=block_decay,
                          final_norm=final_norm),
        grid=(total + 1,),
        in_specs=[
            pl.BlockSpec((None, STEP, D_MODEL), lambda g: (*prev(g), 0)),
            pl.BlockSpec((None, STEP, P_COLS), lambda g: (*cur(g), 0)),
            pl.BlockSpec((None, STEP, 2 * GLA_QK), lambda g: (*cur(g), 0)),
            pl.BlockSpec((None, SUBS, GLA_QK, GLA_DV), lambda g: (*cur(g), 0, 0)),
            pl.BlockSpec((None, SUBS, RET_HEADS, RET_DK, RET_DV), lambda g: (*cur(g), 0, 0, 0)),
            _const_spec((RET_HEADS, BLOCK, BLOCK)),
            _const_spec((BLOCK, RET_QK)),
            _const_spec((BLOCK, RET_QK)),
            _const_spec((BLOCK, RET_QK)),
            _const_spec((1, GLA_DV)),
            _const_spec((1, RET_V)),
            _const_spec((1, RET_V)),
            _const_spec((D_MODEL, D_MODEL)),
            _const_spec((1, D_MODEL)),
            _const_spec((D_MODEL, D_FF)),
            _const_spec((D_FF, D_MODEL)),
            _const_spec((1, D_MODEL)),
        ],
        out_specs=pl.BlockSpec((None, STEP, D_MODEL), lambda g: (*prev(g), 0)),
        out_shape=jax.ShapeDtypeStruct((batch, seq_len, D_MODEL), F32),
        scratch_shapes=[pltpu.VMEM((GLA_QK, GLA_DV), F32),
                        pltpu.VMEM((RET_HEADS, RET_DK, RET_DV), F32),
                        pltpu.VMEM((STEP, D_MODEL), BF16)],
        compiler_params=pltpu.CompilerParams(dimension_semantics=("arbitrary",),
                                             vmem_limit_bytes=VMEM_LIMIT_BYTES),
        name="mix_mlp",
        interpret=interpret,
    )(x, proj, cg, ug, ur, dmat, df, db, ef, lw["gnw"], lw["rnw"], lw["rnb"], lw["w_out"],
      lw["mnw"], lw["w1"], lw["w2"], fnw)


def _trunk(x, layers, fnw, interpret=False):
    for i, lw in enumerate(layers):
        x = _layer(x, lw, fnw, final_norm=(i == len(layers) - 1), interpret=interpret)
    return x


def kernel(x_prompt, x_sample, attn_norm_w, w_in, w_alpha_fwd, b_alpha_fwd, w_alpha_bwd, b_alpha_bwd,
           gla_norm_w, ret_norm_w, ret_norm_b, w_out, mlp_norm_w, w_ff1, w_ff2, final_norm_w):
    stacked = (attn_norm_w, w_in, w_alpha_fwd, b_alpha_fwd, w_alpha_bwd, b_alpha_bwd,
               gla_norm_w, ret_norm_w, ret_norm_b, w_out, mlp_norm_w, w_ff1, w_ff2)
    layers = [_prepare_layer(*(w[l] for w in stacked)) for l in range(w_in.shape[0])]
    fnw = final_norm_w[None, :].astype(F32)
    return _trunk(x_prompt, layers, fnw), _trunk(x_sample, layers, fnw)
```

```python
import functools

import numpy as np
import jax
import jax.numpy as jnp
from jax import lax
from jax.experimental import pallas as pl
from jax.experimental.pallas import tpu as pltpu

F32 = jnp.float32
BF16 = jnp.bfloat16

D_MODEL = 1024
GLA_HEADS = 4
GLA_DK = 64
GLA_DV = 128
GLA_QK = GLA_HEADS * GLA_DK
GLA_V = GLA_HEADS * GLA_DV
GLA_LOWRANK = 16
GLA_NORMALIZER = 16.0
LOG_GATE_MIN = -1.0
RET_HEADS = 4
RET_DK = 128
RET_DV = 128
RET_QK = RET_HEADS * RET_DK
RET_V = RET_HEADS * RET_DV
D_FF = 4 * D_MODEL
ROPE_BASE = 10000.0
EPS = 1e-6

LANES = 128
BLOCK = 256
SUBS = 2
STEP = SUBS * BLOCK
CHUNK = 64
N_CHUNKS = BLOCK // CHUNK
FF_TILE = 512
VMEM_LIMIT_BYTES = 56 * 1024 * 1024

GQ, GK, GV, GG = 0, 256, 512, 1024
RQ, RK, RV, RG = 1536, 2048, 2560, 3072
P_COLS = 3584
GA = P_COLS
W_IN_COLS = P_COLS + LANES

NT = (((1,), (1,)), ((), ()))
TN = (((0,), (0,)), ((), ()))


def _dot(a, b, dims=None):
    if dims is None:
        return jnp.dot(a, b, preferred_element_type=F32)
    return lax.dot_general(a, b, dims, preferred_element_type=F32)


def _rms(x, w):
    return x * lax.rsqrt(jnp.mean(x * x, axis=-1, keepdims=True) + EPS) * w


def _silu(g):
    return g * jax.nn.sigmoid(g)


def _split3(a):
    h1 = a.astype(BF16)
    r1 = a - h1.astype(F32)
    h2 = r1.astype(BF16)
    r2 = r1 - h2.astype(F32)
    return h1, h2, r2.astype(BF16)


def _gla_state_increment(ke, v):
    blocks = []
    for pair in range(GLA_HEADS // 2):
        full = _dot(ke[:, pair * LANES:(pair + 1) * LANES],
                    v[:, pair * 2 * GLA_DV:(pair + 1) * 2 * GLA_DV], TN)
        blocks += [full[:GLA_DK, :GLA_DV], full[GLA_DK:, GLA_DV:]]
    return jnp.concatenate(blocks, axis=0)


def _as_column(row):
    return jnp.broadcast_to(row, (LANES, GLA_QK)).T


def _proj_kernel(x_ref, base_ref, rot_ref, anw_ref, win_ref, wal_ref, bal_ref, eb_ref,
                 proj_ref, cg_ref, ug_ref, ur_ref, sg_ref, sr_ref, *, ret_block_decay):
    @pl.when(pl.program_id(1) == 0)
    def _():
        sg_ref[...] = jnp.zeros_like(sg_ref)
        sr_ref[...] = jnp.zeros_like(sr_ref)

    cos_b, sin_b = base_ref[0:1, :], base_ref[1:2, :]
    cos_i, sin_i = rot_ref[0], rot_ref[1]
    half_sign = jnp.where(lax.broadcasted_iota(jnp.int32, (1, LANES), 1) < RET_DK // 2, -1.0, 1.0)
    cos_all = cos_b * cos_i - sin_b * sin_i
    sin_all = (sin_b * cos_i + cos_b * sin_i) * half_sign

    row = lax.broadcasted_iota(jnp.int32, (BLOCK, BLOCK), 0)
    col = lax.broadcasted_iota(jnp.int32, (BLOCK, BLOCK), 1)
    tri_f = jnp.where(col <= row, 1.0, 0.0).astype(BF16)
    tri_b = jnp.where(col >= row, 1.0, 0.0).astype(BF16)

    def block(sub):
        rows = slice(sub * BLOCK, (sub + 1) * BLOCK)
        h = _rms(x_ref[rows, :], anw_ref[...]).astype(BF16)

        def proj(lo, width):
            return _dot(h, win_ref[:, lo:lo + width])

        rg_ga = proj(RG, RET_V + LANES)
        proj_ref[rows, RG:RG + RET_V] = rg_ga[:, :RET_V].astype(BF16)
        ga = rg_ga[:, RET_V:].astype(BF16)
        z = _dot(ga, wal_ref[...]) + bal_ref[...]
        la = jnp.maximum(jax.nn.log_sigmoid(z) * (1.0 / GLA_NORMALIZER), LOG_GATE_MIN)
        la_f = _split3(la[:, :GLA_QK])
        la_b = _split3(la[:, GLA_QK:])

        gqk = proj(GQ, 2 * GLA_QK)
        proj_ref[rows, GQ:GQ + 2 * GLA_QK] = gqk.astype(BF16)
        gk = gqk[:, GLA_QK:]
        gv = proj(GV, GLA_V).astype(BF16)
        proj_ref[rows, GV:GV + GLA_V] = gv
        proj_ref[rows, GG:GG + GLA_V] = proj(GG, GLA_V).astype(BF16)
        rq_all = proj(RQ, RET_QK)
        rk_all = proj(RK, RET_QK)
        rv_all = proj(RV, RET_V).astype(BF16)
        proj_ref[rows, RV:RV + RET_V] = rv_all

        bf = sum(_dot(tri_f, p) for p in la_f)
        cb = sum(_dot(tri_b, p) for p in la_b)
        cg_ref[rows, :GLA_QK] = bf
        cg_ref[rows, GLA_QK:] = cb

        cos = cos_all[rows, :]
        sin = sin_all[rows, :]
        ur_ref[sub] = sr_ref[...].astype(BF16)
        for hh in range(RET_HEADS):
            lo = hh * RET_DK
            q = rq_all[:, lo:lo + RET_DK]
            k = rk_all[:, lo:lo + RET_DK]
            q = q * cos + pltpu.roll(q, RET_DK // 2, axis=1) * sin
            k = k * cos + pltpu.roll(k, RET_DK // 2, axis=1) * sin
            proj_ref[rows, RQ + lo:RQ + lo + RET_DK] = q.astype(BF16)
            proj_ref[rows, RK + lo:RK + lo + RET_DK] = k.astype(BF16)
            v = rv_all[:, lo:lo + RET_DV]
            kd = (k * eb_ref[:, lo:lo + RET_DK]).astype(BF16)
            sr_ref[hh] = sr_ref[hh] * ret_block_decay[hh] + _dot(kd, v, TN)

        ug_ref[sub] = sg_ref[...].astype(BF16)
        ctot = cb[0:1, :]
        kb = (gk * jnp.exp(ctot - cb)).astype(BF16)
        sg_ref[...] = sg_ref[...] * _as_column(jnp.exp(ctot)) + _gla_state_increment(kb, gv)

    for sub in reversed(range(SUBS)):
        block(sub)


def _mix_kernel(xprev_ref, proj_ref, cg_ref, ug_ref, ur_ref, dmat_ref, df_ref, db_ref, ef_ref,
                gnw_ref, rnw_ref, rnb_ref, wout_ref, mnw_ref, w1_ref, w2_ref, fnw_ref,
                y_ref, fg_ref, fr_ref, mixed_ref, *, steps_per_seq, ret_block_decay, final_norm):
    g = pl.program_id(0)

    @pl.when(g % steps_per_seq == 0)
    def _():
        fg_ref[...] = jnp.zeros_like(fg_ref)
        fr_ref[...] = jnp.zeros_like(fr_ref)

    @pl.when(g == 0)
    def _():
        mixed_ref[...] = jnp.zeros_like(mixed_ref)

    dense = {}

    def dense_head():
        x1 = xprev_ref[...] + _dot(mixed_ref[...], wout_ref[...])
        dense["h2"] = _rms(x1, mnw_ref[...]).astype(BF16)
        dense["acc"] = x1

    def dense_tile(f):
        ff = jnp.maximum(_dot(dense["h2"], w1_ref[:, f * FF_TILE:(f + 1) * FF_TILE]), 0.0)
        dense["acc"] = dense["acc"] + _dot((ff * ff).astype(BF16),
                                           w2_ref[f * FF_TILE:(f + 1) * FF_TILE, :])

    def dense_finish():
        y_ref[...] = _rms(dense["acc"], fnw_ref[...]) if final_norm else dense["acc"]

    n_ff = D_FF // FF_TILE
    dense_head()

    def gla_operands(sub, a, forward):
        r0 = sub * BLOCK + a * CHUNK
        if forward:
            cum = cg_ref[r0:r0 + CHUNK, 0:GLA_QK]
            if a > 0:
                cum = cum - cg_ref[r0 - 1:r0, 0:GLA_QK]
        else:
            cum = cg_ref[r0:r0 + CHUNK, GLA_QK:2 * GLA_QK]
            if a < N_CHUNKS - 1:
                cum = cum - cg_ref[r0 + CHUNK:r0 + CHUNK + 1, GLA_QK:2 * GLA_QK]
        e = jnp.exp(cum)
        e_edge = e[CHUNK - 1:CHUNK, :] if forward else e[0:1, :]
        q = proj_ref[r0:r0 + CHUNK, GQ:GQ + GLA_QK].astype(F32)
        k = proj_ref[r0:r0 + CHUNK, GK:GK + GLA_QK].astype(F32)
        ki = k * jnp.exp(-cum)
        return dict(qd=(q * e).astype(BF16), ki=ki.astype(BF16), ke=(ki * e_edge).astype(BF16),
                    e_edge=e_edge, v=proj_ref[r0:r0 + CHUNK, GV:GV + GLA_V])

    sweeps = {True: list(range(N_CHUNKS)), False: list(reversed(range(N_CHUNKS)))}
    ops = {(sub, fw, a): gla_operands(sub, a, fw)
           for sub in range(SUBS) for fw in (True, False) for a in sweeps[fw]}

    ret = {}
    for sub in range(SUBS):
        rows = slice(sub * BLOCK, (sub + 1) * BLOCK)
        for h in range(RET_HEADS):
            lo = h * RET_DK
            q = proj_ref[rows, RQ + lo:RQ + lo + RET_DK]
            k = proj_ref[rows, RK + lo:RK + lo + RET_DK]
            qf = q.astype(F32)
            ret[(sub, h)] = dict(
                q=q, k=k, v=proj_ref[rows, RV + lo:RV + lo + RET_DV],
                q_both=jnp.concatenate([(qf * df_ref[:, lo:lo + RET_DK]).astype(BF16),
                                        (qf * db_ref[:, lo:lo + RET_DK]).astype(BF16)], axis=1),
                kd=(k.astype(F32) * ef_ref[:, lo:lo + RET_DK]).astype(BF16))

    n_pairs = GLA_HEADS // 2
    lane = lax.broadcasted_iota(jnp.int32, (CHUNK, LANES), 1)
    even_lanes = lane < GLA_DK
    for key, op in ops.items():
        _, fw, a = key
        if fw or a > 0:
            op["inc"] = _gla_state_increment(op["ke"], op["v"])
        op["s"] = []
        for pair in range(n_pairs):
            ki = op["ki"][:, pair * LANES:(pair + 1) * LANES]
            keys = jnp.concatenate([jnp.where(even_lanes, 0.0, ki), jnp.where(even_lanes, ki, 0.0)],
                                   axis=0).astype(BF16)
            op["s"].append(_dot(op["qd"][:, pair * LANES:(pair + 1) * LANES], keys, NT))
    for r in ret.values():
        r["inc"] = _dot(r["kd"], r["v"], TN)
        r["s"] = _dot(r["q"], r["k"], NT)
    for h in range(RET_HEADS):
        state = fr_ref[h]
        for sub in range(SUBS):
            r = ret[(sub, h)]
            states = jnp.concatenate([state.astype(BF16), ur_ref[sub, h]], axis=0)
            r["inter"] = _dot(r["q_both"], states)
            state = state * ret_block_decay[h] + r["inc"]
        fr_ref[h] = state

    for f in range(n_ff // 2):
        dense_tile(f)

    ri = lax.broadcasted_iota(jnp.int32, (CHUNK, LANES), 0)
    ci = lane % CHUNK
    state = fg_ref[...]
    for sub in range(SUBS):
        for fw in (True, False):
            if not fw:
                fwd_state, state = state, ug_ref[sub].astype(F32)
            keep = (ci <= ri) if fw else (ci > ri)
            for a in sweeps[fw]:
                op = ops[(sub, fw, a)]
                op["state"] = state.astype(BF16)
                op["p"] = [jnp.where(keep, s, 0.0).astype(BF16) for s in op["s"]]
                if "inc" in op:
                    state = state * _as_column(op["e_edge"]) + op["inc"]
        state = fwd_state
    fg_ref[...] = state
    for (sub, h), r in ret.items():
        r["p"] = (r["s"] * dmat_ref[h]).astype(BF16)

    for op in ops.values():
        outs = []
        for h in range(GLA_HEADS):
            pair, odd = divmod(h, 2)
            qd = op["qd"][:, pair * LANES:(pair + 1) * LANES]
            p = op["p"][pair]
            v = op["v"][:, h * GLA_DV:(h + 1) * GLA_DV]
            s_h = op["state"][h * GLA_DK:(h + 1) * GLA_DK, :]
            if odd:
                lhs = jnp.where(even_lanes, p, qd)
                rhs = jnp.concatenate([v, s_h], axis=0)
            else:
                lhs = jnp.where(even_lanes, qd, p)
                rhs = jnp.concatenate([s_h, v], axis=0)
            outs.append(_dot(lhs, rhs))
        op["o"] = jnp.concatenate(outs, axis=1)
    for r in ret.values():
        r["o"] = _dot(r["p"], r["v"]) + r["inter"]

    for f in range(n_ff // 2, n_ff):
        dense_tile(f)

    gnw = gnw_ref[...]
    for sub in range(SUBS):
        rows = slice(sub * BLOCK, (sub + 1) * BLOCK)
        o_gla = jnp.concatenate([ops[(sub, True, a)]["o"] + ops[(sub, False, a)]["o"]
                                 for a in range(N_CHUNKS)], axis=0)
        mixed = []
        for h in range(GLA_HEADS):
            lo = h * GLA_DV
            oh = _rms(o_gla[:, lo:lo + GLA_DV], gnw)
            mixed.append(oh * _silu(proj_ref[rows, GG + lo:GG + lo + GLA_DV].astype(F32)))
        for h in range(RET_HEADS):
            lo = h * RET_DV
            o = ret[(sub, h)]["o"]
            mu = jnp.mean(o, axis=-1, keepdims=True)
            oc = o - mu
            var = jnp.mean(oc * oc, axis=-1, keepdims=True)
            on = oc * lax.rsqrt(var + EPS) * rnw_ref[:, lo:lo + RET_DV] + rnb_ref[:, lo:lo + RET_DV]
            mixed.append(on * _silu(proj_ref[rows, RG + lo:RG + lo + RET_DV].astype(F32)))
        mixed_ref[rows, :] = jnp.concatenate(mixed, axis=1).astype(BF16)
    dense_finish()


def _const_spec(shape):
    zeros = (0,) * len(shape)
    return pl.BlockSpec(shape, lambda *_: zeros, pipeline_mode=pl.Buffered(1))


def _retention_tables():
    log_gamma = np.log1p(-np.power(2.0, -5.0 - np.arange(RET_HEADS, dtype=np.float64)))
    i = np.arange(BLOCK, dtype=np.float64)
    scale = RET_DK ** -0.5
    dist = np.abs(i[:, None] - i[None, :])
    dmat = np.exp(log_gamma[:, None, None] * dist[None]) * scale

    def expand(t):
        return np.repeat(t, RET_DK, axis=1).astype(np.float32)

    df = expand(np.exp(np.outer(i + 1.0, log_gamma)) * scale)
    db = expand(np.exp(np.outer(BLOCK - i, log_gamma)) * scale)
    ef = expand(np.exp(np.outer(BLOCK - 1.0 - i, log_gamma)))
    eb = expand(np.exp(np.outer(i, log_gamma)))
    block_decay = tuple(float(g) for g in np.exp(BLOCK * log_gamma))
    return dmat.astype(np.float32), df, db, ef, eb, block_decay


def _rotary_tables(seq_len):
    half = RET_DK // 2
    inv_freq = np.power(ROPE_BASE, -np.arange(half, dtype=np.float64) / half)

    def cos_sin(pos):
        ang = pos[:, None] * inv_freq[None, :]
        return np.stack([np.tile(np.cos(ang), (1, 2)), np.tile(np.sin(ang), (1, 2))]).astype(np.float32)

    base = cos_sin(np.arange(0, seq_len, STEP, dtype=np.float64))
    return np.ascontiguousarray(base.transpose(1, 0, 2)), cos_sin(np.arange(STEP, dtype=np.float64))


def _prepare_layer(attn_norm_w, w_in, w_alpha_fwd, b_alpha_fwd, w_alpha_bwd, b_alpha_bwd,
                   gla_norm_w, ret_norm_w, ret_norm_b, w_out, mlp_norm_w, w_ff1, w_ff2):
    o = np.cumsum((0, GLA_QK, GLA_QK, GLA_V, GLA_V, GLA_LOWRANK, GLA_LOWRANK,
                   RET_QK, RET_QK, RET_V, RET_V))
    gq, gk, gv, gg, ga_f, ga_b, rq, rk, rv, rg = (w_in[:, o[i]:o[i + 1]] for i in range(10))
    pad = jnp.zeros((D_MODEL, LANES - 2 * GLA_LOWRANK), w_in.dtype)
    w_in_p = jnp.concatenate([gq * (GLA_DK ** -0.5), gk, gv, gg, rq, rk, rv, rg, ga_f, ga_b, pad],
                             axis=1).astype(BF16)
    w_al = jnp.zeros((LANES, 2 * GLA_QK), F32)
    w_al = w_al.at[:GLA_LOWRANK, :GLA_QK].set(w_alpha_fwd)
    w_al = w_al.at[GLA_LOWRANK:2 * GLA_LOWRANK, GLA_QK:].set(w_alpha_bwd)
    b_al = jnp.concatenate([b_alpha_fwd, b_alpha_bwd])[None, :].astype(F32)
    return dict(
        anw=attn_norm_w[None, :].astype(F32), w_in=w_in_p, w_al=w_al.astype(BF16), b_al=b_al,
        gnw=gla_norm_w[None, :].astype(F32), rnw=ret_norm_w[None, :].astype(F32),
        rnb=ret_norm_b[None, :].astype(F32), w_out=w_out.astype(BF16),
        mnw=mlp_norm_w[None, :].astype(F32), w1=w_ff1.astype(BF16), w2=w_ff2.astype(BF16))


def _layer(x, lw, fnw, final_norm, interpret=False):
    batch, seq_len, _ = x.shape
    assert seq_len % STEP == 0
    ns = seq_len // STEP
    nb = seq_len // BLOCK
    dmat, df, db, ef, eb, block_decay = _retention_tables()
    rot_base, rot_step = _rotary_tables(seq_len)

    def rev(b, n):
        return (b, ns - 1 - n, 0)

    proj, cg, ug, ur = pl.pallas_call(
        functools.partial(_proj_kernel, ret_block_decay=block_decay),
        grid=(batch, ns),
        in_specs=[
            pl.BlockSpec((None, STEP, D_MODEL), rev),
            pl.BlockSpec((None, 2, LANES), lambda b, n: (ns - 1 - n, 0, 0)),
            _const_spec((2, STEP, LANES)),
            _const_spec((1, D_MODEL)),
            _const_spec((D_MODEL, W_IN_COLS)),
            _const_spec((LANES, 2 * GLA_QK)),
            _const_spec((1, 2 * GLA_QK)),
            _const_spec((BLOCK, RET_QK)),
        ],
        out_specs=[
            pl.BlockSpec((None, STEP, P_COLS), rev),
            pl.BlockSpec((None, STEP, 2 * GLA_QK), rev),
            pl.BlockSpec((None, SUBS, GLA_QK, GLA_DV), lambda b, n: (b, ns - 1 - n, 0, 0)),
            pl.BlockSpec((None, SUBS, RET_HEADS, RET_DK, RET_DV),
                         lambda b, n: (b, ns - 1 - n, 0, 0, 0)),
        ],
        out_shape=[
            jax.ShapeDtypeStruct((batch, seq_len, P_COLS), BF16),
            jax.ShapeDtypeStruct((batch, seq_len, 2 * GLA_QK), F32),
            jax.ShapeDtypeStruct((batch, nb, GLA_QK, GLA_DV), BF16),
            jax.ShapeDtypeStruct((batch, nb, RET_HEADS, RET_DK, RET_DV), BF16),
        ],
        scratch_shapes=[pltpu.VMEM((GLA_QK, GLA_DV), F32),
                        pltpu.VMEM((RET_HEADS, RET_DK, RET_DV), F32)],
        compiler_params=pltpu.CompilerParams(dimension_semantics=("arbitrary", "arbitrary"),
                                             vmem_limit_bytes=VMEM_LIMIT_BYTES),
        name="proj_bwd_state",
        interpret=interpret,
    )(x, rot_base, rot_step, lw["anw"], lw["w_in"], lw["w_al"], lw["b_al"], eb)

    total = batch * ns

    def cur(g):
        s = jnp.minimum(g, total - 1)
        return s // ns, s % ns

    def prev(g):
        s = jnp.maximum(g - 1, 0)
        return s // ns, s % ns

    return pl.pallas_call(
        functools.partial(_mix_kernel, steps_per_seq=ns, ret_block_decay=block_decay,
                          final_norm=final_norm),
        grid=(total + 1,),
        in_specs=[
            pl.BlockSpec((None, STEP, D_MODEL), lambda g: (*prev(g), 0)),
            pl.BlockSpec((None, STEP, P_COLS), lambda g: (*cur(g), 0)),
            pl.BlockSpec((None, STEP, 2 * GLA_QK), lambda g: (*cur(g), 0)),
            pl.BlockSpec((None, SUBS, GLA_QK, GLA_DV), lambda g: (*cur(g), 0, 0)),
            pl.BlockSpec((None, SUBS, RET_HEADS, RET_DK, RET_DV), lambda g: (*cur(g), 0, 0, 0)),
            _const_spec((RET_HEADS, BLOCK, BLOCK)),
            _const_spec((BLOCK, RET_QK)),
            _const_spec((BLOCK, RET_QK)),
            _const_spec((BLOCK, RET_QK)),
            _const_spec((1, GLA_DV)),
            _const_spec((1, RET_V)),
            _const_spec((1, RET_V)),
            _const_spec((D_MODEL, D_MODEL)),
            _const_spec((1, D_MODEL)),
            _const_spec((D_MODEL, D_FF)),
            _const_spec((D_FF, D_MODEL)),
            _const_spec((1, D_MODEL)),
        ],
        out_specs=pl.BlockSpec((None, STEP, D_MODEL), lambda g: (*prev(g), 0)),
        out_shape=jax.ShapeDtypeStruct((batch, seq_len, D_MODEL), F32),
        scratch_shapes=[pltpu.VMEM((GLA_QK, GLA_DV), F32),
                        pltpu.VMEM((RET_HEADS, RET_DK, RET_DV), F32),
                        pltpu.VMEM((STEP, D_MODEL), BF16)],
        compiler_params=pltpu.CompilerParams(dimension_semantics=("arbitrary",),
                                             vmem_limit_bytes=VMEM_LIMIT_BYTES),
        name="mix_mlp",
        interpret=interpret,
    )(x, proj, cg, ug, ur, dmat, df, db, ef, lw["gnw"], lw["rnw"], lw["rnb"], lw["w_out"],
      lw["mnw"], lw["w1"], lw["w2"], fnw)


def _trunk(x, layers, fnw, interpret=False):
    for i, lw in enumerate(layers):
        x = _layer(x, lw, fnw, final_norm=(i == len(layers) - 1), interpret=interpret)
    return x


def kernel(x_prompt, x_sample, attn_norm_w, w_in, w_alpha_fwd, b_alpha_fwd, w_alpha_bwd, b_alpha_bwd,
           gla_norm_w, ret_norm_w, ret_norm_b, w_out, mlp_norm_w, w_ff1, w_ff2, final_norm_w):
    stacked = (attn_norm_w, w_in, w_alpha_fwd, b_alpha_fwd, w_alpha_bwd, b_alpha_bwd,
               gla_norm_w, ret_norm_w, ret_norm_b, w_out, mlp_norm_w, w_ff1, w_ff2)
    layers = [_prepare_layer(*(w[l] for w in stacked)) for l in range(w_in.shape[0])]
    fnw = final_norm_w[None, :].astype(F32)
    return _trunk(x_prompt, layers, fnw), _trunk(x_sample, layers, fnw)
```

```python
import functools

import numpy as np
import jax
import jax.numpy as jnp
from jax import lax
from jax.experimental import pallas as pl
from jax.experimental.pallas import tpu as pltpu

F32 = jnp.float32
BF16 = jnp.bfloat16

D_MODEL = 1024
GLA_HEADS = 4
GLA_DK = 64
GLA_DV = 128
GLA_QK = GLA_HEADS * GLA_DK
GLA_V = GLA_HEADS * GLA_DV
GLA_LOWRANK = 16
GLA_NORMALIZER = 16.0
LOG_GATE_MIN = -1.0
RET_HEADS = 4
RET_DK = 128
RET_DV = 128
RET_QK = RET_HEADS * RET_DK
RET_V = RET_HEADS * RET_DV
D_FF = 4 * D_MODEL
ROPE_BASE = 10000.0
EPS = 1e-6

LANES = 128
BLOCK = 256
SUBS = 2
STEP = SUBS * BLOCK
CHUNK = 64
N_CHUNKS = BLOCK // CHUNK
FF_TILE = 512
VMEM_LIMIT_BYTES = 56 * 1024 * 1024

GQ, GK, GV, GG = 0, 256, 512, 1024
RQ, RK, RV, RG = 1536, 2048, 2560, 3072
P_COLS = 3584
GA = P_COLS
W_IN_COLS = P_COLS + LANES

NT = (((1,), (1,)), ((), ()))
TN = (((0,), (0,)), ((), ()))


def _dot(a, b, dims=None):
    if dims is None:
        return jnp.dot(a, b, preferred_element_type=F32)
    return lax.dot_general(a, b, dims, preferred_element_type=F32)


def _rms(x, w):
    return x * lax.rsqrt(jnp.mean(x * x, axis=-1, keepdims=True) + EPS) * w


def _silu(g):
    return g * jax.nn.sigmoid(g)


def _split3(a):
    h1 = a.astype(BF16)
    r1 = a - h1.astype(F32)
    h2 = r1.astype(BF16)
    r2 = r1 - h2.astype(F32)
    return h1, h2, r2.astype(BF16)


def _gla_state_increment(ke, v):
    blocks = []
    for pair in range(GLA_HEADS // 2):
        full = _dot(ke[:, pair * LANES:(pair + 1) * LANES],
                    v[:, pair * 2 * GLA_DV:(pair + 1) * 2 * GLA_DV], TN)
        blocks += [full[:GLA_DK, :GLA_DV], full[GLA_DK:, GLA_DV:]]
    return jnp.concatenate(blocks, axis=0)


def _as_column(row):
    return jnp.broadcast_to(row, (LANES, GLA_QK)).T


def _proj_kernel(x_ref, base_ref, rot_ref, anw_ref, win_ref, wal_ref, bal_ref, eb_ref,
                 proj_ref, cg_ref, ug_ref, ur_ref, sg_ref, sr_ref, *, ret_block_decay):
    @pl.when(pl.program_id(1) == 0)
    def _():
        sg_ref[...] = jnp.zeros_like(sg_ref)
        sr_ref[...] = jnp.zeros_like(sr_ref)

    cos_b, sin_b = base_ref[0:1, :], base_ref[1:2, :]
    cos_i, sin_i = rot_ref[0], rot_ref[1]
    half_sign = jnp.where(lax.broadcasted_iota(jnp.int32, (1, LANES), 1) < RET_DK // 2, -1.0, 1.0)
    cos_all = cos_b * cos_i - sin_b * sin_i
    sin_all = (sin_b * cos_i + cos_b * sin_i) * half_sign

    row = lax.broadcasted_iota(jnp.int32, (CHUNK, 3 * CHUNK), 0)
    col = lax.broadcasted_iota(jnp.int32, (CHUNK, 3 * CHUNK), 1) % CHUNK
    tri_f = jnp.where(col <= row, 1.0, 0.0).astype(BF16)
    tri_b = jnp.where(col >= row, 1.0, 0.0).astype(BF16)

    def chunk_cumsums(tri, terms):
        return [_dot(tri, jnp.concatenate([t[a * CHUNK:(a + 1) * CHUNK] for t in terms], axis=0))
                for a in range(N_CHUNKS)]

    def block(sub):
        rows = slice(sub * BLOCK, (sub + 1) * BLOCK)
        h = _rms(x_ref[rows, :], anw_ref[...]).astype(BF16)

        def proj(lo, width):
            return _dot(h, win_ref[:, lo:lo + width])

        rg_ga = proj(RG, RET_V + LANES)
        proj_ref[rows, RG:RG + RET_V] = rg_ga[:, :RET_V].astype(BF16)
        ga = rg_ga[:, RET_V:].astype(BF16)
        z = _dot(ga, wal_ref[...]) + bal_ref[...]
        la = jnp.maximum(jax.nn.log_sigmoid(z) * (1.0 / GLA_NORMALIZER), LOG_GATE_MIN)
        la_f = _split3(la[:, :GLA_QK])
        la_b = _split3(la[:, GLA_QK:])

        gqk = proj(GQ, 2 * GLA_QK)
        proj_ref[rows, GQ:GQ + 2 * GLA_QK] = gqk.astype(BF16)
        gk = gqk[:, GLA_QK:]
        gv = proj(GV, GLA_V).astype(BF16)
        proj_ref[rows, GV:GV + GLA_V] = gv
        proj_ref[rows, GG:GG + GLA_V] = proj(GG, GLA_V).astype(BF16)
        rq_all = proj(RQ, RET_QK)
        rk_all = proj(RK, RET_QK)
        rv_all = proj(RV, RET_V).astype(BF16)
        proj_ref[rows, RV:RV + RET_V] = rv_all

        cum_b = chunk_cumsums(tri_b, la_b)
        cg_ref[rows, :GLA_QK] = jnp.concatenate(chunk_cumsums(tri_f, la_f), axis=0)
        cg_ref[rows, GLA_QK:] = jnp.concatenate(cum_b, axis=0)
        offset = jnp.zeros((1, GLA_QK), F32)
        for a in reversed(range(N_CHUNKS)):
            cum_b[a], offset = cum_b[a] + offset, offset + cum_b[a][0:1, :]
        cb = jnp.concatenate(cum_b, axis=0)

        cos = cos_all[rows, :]
        sin = sin_all[rows, :]
        ur_ref[sub] = sr_ref[...].astype(BF16)
        for hh in range(RET_HEADS):
            lo = hh * RET_DK
            q = rq_all[:, lo:lo + RET_DK]
            k = rk_all[:, lo:lo + RET_DK]
            q = q * cos + pltpu.roll(q, RET_DK // 2, axis=1) * sin
            k = k * cos + pltpu.roll(k, RET_DK // 2, axis=1) * sin
            proj_ref[rows, RQ + lo:RQ + lo + RET_DK] = q.astype(BF16)
            proj_ref[rows, RK + lo:RK + lo + RET_DK] = k.astype(BF16)
            v = rv_all[:, lo:lo + RET_DV]
            kd = (k * eb_ref[:, lo:lo + RET_DK]).astype(BF16)
            sr_ref[hh] = sr_ref[hh] * ret_block_decay[hh] + _dot(kd, v, TN)

        ug_ref[sub] = sg_ref[...].astype(BF16)
        ctot = cb[0:1, :]
        kb = (gk * jnp.exp(ctot - cb)).astype(BF16)
        sg_ref[...] = sg_ref[...] * _as_column(jnp.exp(ctot)) + _gla_state_increment(kb, gv)

    for sub in reversed(range(SUBS)):
        block(sub)


def _mix_kernel(xprev_ref, proj_ref, cg_ref, ug_ref, ur_ref, dmat_ref, df_ref, db_ref, ef_ref,
                gnw_ref, rnw_ref, rnb_ref, wout_ref, mnw_ref, w1_ref, w2_ref, fnw_ref,
                y_ref, fg_ref, fr_ref, mixed_ref, *, steps_per_seq, ret_block_decay, final_norm):
    g = pl.program_id(0)

    @pl.when(g % steps_per_seq == 0)
    def _():
        fg_ref[...] = jnp.zeros_like(fg_ref)
        fr_ref[...] = jnp.zeros_like(fr_ref)

    @pl.when(g == 0)
    def _():
        mixed_ref[...] = jnp.zeros_like(mixed_ref)

    dense = {}

    def dense_head():
        x1 = xprev_ref[...] + _dot(mixed_ref[...], wout_ref[...])
        dense["h2"] = _rms(x1, mnw_ref[...]).astype(BF16)
        dense["acc"] = x1

    def dense_tile(f):
        ff = jnp.maximum(_dot(dense["h2"], w1_ref[:, f * FF_TILE:(f + 1) * FF_TILE]), 0.0)
        dense["acc"] = dense["acc"] + _dot((ff * ff).astype(BF16),
                                           w2_ref[f * FF_TILE:(f + 1) * FF_TILE, :])

    def dense_finish():
        y_ref[...] = _rms(dense["acc"], fnw_ref[...]) if final_norm else dense["acc"]

    n_ff = D_FF // FF_TILE
    dense_head()

    def gla_operands(sub, a, forward):
        r0 = sub * BLOCK + a * CHUNK
        cum = cg_ref[r0:r0 + CHUNK, 0:GLA_QK] if forward else cg_ref[r0:r0 + CHUNK, GLA_QK:2 * GLA_QK]
        e = jnp.exp(cum)
        e_edge = e[CHUNK - 1:CHUNK, :] if forward else e[0:1, :]
        q = proj_ref[r0:r0 + CHUNK, GQ:GQ + GLA_QK].astype(F32)
        k = proj_ref[r0:r0 + CHUNK, GK:GK + GLA_QK].astype(F32)
        ki = k * jnp.exp(-cum)
        return dict(qd=(q * e).astype(BF16), ki=ki, ke=(ki * e_edge).astype(BF16),
                    e_edge=e_edge, v=proj_ref[r0:r0 + CHUNK, GV:GV + GLA_V])

    sweeps = {True: list(range(N_CHUNKS)), False: list(reversed(range(N_CHUNKS)))}
    ops = {(sub, fw, a): gla_operands(sub, a, fw)
           for sub in range(SUBS) for fw in (True, False) for a in sweeps[fw]}

    ret = {}
    for sub in range(SUBS):
        rows = slice(sub * BLOCK, (sub + 1) * BLOCK)
        for h in range(RET_HEADS):
            lo = h * RET_DK
            q = proj_ref[rows, RQ + lo:RQ + lo + RET_DK]
            k = proj_ref[rows, RK + lo:RK + lo + RET_DK]
            qf = q.astype(F32)
            ret[(sub, h)] = dict(
                q=q, k=k, v=proj_ref[rows, RV + lo:RV + lo + RET_DV],
                q_both=jnp.concatenate([(qf * df_ref[:, lo:lo + RET_DK]).astype(BF16),
                                        (qf * db_ref[:, lo:lo + RET_DK]).astype(BF16)], axis=1),
                kd=(k.astype(F32) * ef_ref[:, lo:lo + RET_DK]).astype(BF16))

    n_pairs = GLA_HEADS // 2
    lane = lax.broadcasted_iota(jnp.int32, (CHUNK, LANES), 1)
    even_lanes = lane < GLA_DK
    for key, op in ops.items():
        _, fw, a = key
        if fw or a > 0:
            op["inc"] = _gla_state_increment(op["ke"], op["v"])
        op["s"] = []
        for pair in range(n_pairs):
            ki = op["ki"][:, pair * LANES:(pair + 1) * LANES]
            keys = jnp.concatenate([jnp.where(even_lanes, 0.0, ki), jnp.where(even_lanes, ki, 0.0)],
                                   axis=0)
            op["s"].append(_dot(op["qd"][:, pair * LANES:(pair + 1) * LANES], keys.T.astype(BF16)))
    for r in ret.values():
        r["inc"] = _dot(r["kd"], r["v"], TN)
        r["s"] = _dot(r["q"], r["k"], NT)
    for h in range(RET_HEADS):
        state = fr_ref[h]
        for sub in range(SUBS):
            r = ret[(sub, h)]
            states = jnp.concatenate([state.astype(BF16), ur_ref[sub, h]], axis=0)
            r["inter"] = _dot(r["q_both"], states)
            state = state * ret_block_decay[h] + r["inc"]
        fr_ref[h] = state

    for f in range(n_ff // 2):
        dense_tile(f)

    ri = lax.broadcasted_iota(jnp.int32, (CHUNK, LANES), 0)
    ci = lane % CHUNK
    state = fg_ref[...]
    for sub in range(SUBS):
        for fw in (True, False):
            if not fw:
                fwd_state, state = state, ug_ref[sub].astype(F32)
            keep = (ci <= ri) if fw else (ci > ri)
            for a in sweeps[fw]:
                op = ops[(sub, fw, a)]
                op["state"] = state.astype(BF16)
                op["p"] = [jnp.where(keep, s, 0.0).astype(BF16) for s in op["s"]]
                if "inc" in op:
                    state = state * _as_column(op["e_edge"]) + op["inc"]
        state = fwd_state
    fg_ref[...] = state
    for (sub, h), r in ret.items():
        r["p"] = (r["s"] * dmat_ref[h]).astype(BF16)

    for op in ops.values():
        outs = []
        for h in range(GLA_HEADS):
            pair, odd = divmod(h, 2)
            qd = op["qd"][:, pair * LANES:(pair + 1) * LANES]
            p = op["p"][pair]
            v = op["v"][:, h * GLA_DV:(h + 1) * GLA_DV]
            s_h = op["state"][h * GLA_DK:(h + 1) * GLA_DK, :]
            if odd:
                lhs = jnp.where(even_lanes, p, qd)
                rhs = jnp.concatenate([v, s_h], axis=0)
            else:
                lhs = jnp.where(even_lanes, qd, p)
                rhs = jnp.concatenate([s_h, v], axis=0)
            outs.append(_dot(lhs, rhs))
        op["o"] = jnp.concatenate(outs, axis=1)
    for r in ret.values():
        r["o"] = _dot(r["p"], r["v"]) + r["inter"]

    for f in range(n_ff // 2, n_ff):
        dense_tile(f)

    gnw = gnw_ref[...]
    for sub in range(SUBS):
        rows = slice(sub * BLOCK, (sub + 1) * BLOCK)
        o_gla = jnp.concatenate([ops[(sub, True, a)]["o"] + ops[(sub, False, a)]["o"]
                                 for a in range(N_CHUNKS)], axis=0)
        mixed = []
        for h in range(GLA_HEADS):
            lo = h * GLA_DV
            oh = _rms(o_gla[:, lo:lo + GLA_DV], gnw)
            mixed.append(oh * _silu(proj_ref[rows, GG + lo:GG + lo + GLA_DV].astype(F32)))
        for h in range(RET_HEADS):
            lo = h * RET_DV
            o = ret[(sub, h)]["o"]
            mu = jnp.mean(o, axis=-1, keepdims=True)
            oc = o - mu
            var = jnp.mean(oc * oc, axis=-1, keepdims=True)
            on = oc * lax.rsqrt(var + EPS) * rnw_ref[:, lo:lo + RET_DV] + rnb_ref[:, lo:lo + RET_DV]
            mixed.append(on * _silu(proj_ref[rows, RG + lo:RG + lo + RET_DV].astype(F32)))
        mixed_ref[rows, :] = jnp.concatenate(mixed, axis=1).astype(BF16)
    dense_finish()


def _const_spec(shape):
    zeros = (0,) * len(shape)
    return pl.BlockSpec(shape, lambda *_: zeros, pipeline_mode=pl.Buffered(1))


def _retention_tables():
    log_gamma = np.log1p(-np.power(2.0, -5.0 - np.arange(RET_HEADS, dtype=np.float64)))
    i = np.arange(BLOCK, dtype=np.float64)
    scale = RET_DK ** -0.5
    dist = np.abs(i[:, None] - i[None, :])
    dmat = np.exp(log_gamma[:, None, None] * dist[None]) * scale

    def expand(t):
        return np.repeat(t, RET_DK, axis=1).astype(np.float32)

    df = expand(np.exp(np.outer(i + 1.0, log_gamma)) * scale)
    db = expand(np.exp(np.outer(BLOCK - i, log_gamma)) * scale)
    ef = expand(np.exp(np.outer(BLOCK - 1.0 - i, log_gamma)))
    eb = expand(np.exp(np.outer(i, log_gamma)))
    block_decay = tuple(float(g) for g in np.exp(BLOCK * log_gamma))
    return dmat.astype(np.float32), df, db, ef, eb, block_decay


def _rotary_tables(seq_len):
    half = RET_DK // 2
    inv_freq = np.power(ROPE_BASE, -np.arange(half, dtype=np.float64) / half)

    def cos_sin(pos):
        ang = pos[:, None] * inv_freq[None, :]
        return np.stack([np.tile(np.cos(ang), (1, 2)), np.tile(np.sin(ang), (1, 2))]).astype(np.float32)

    base = cos_sin(np.arange(0, seq_len, STEP, dtype=np.float64))
    return np.ascontiguousarray(base.transpose(1, 0, 2)), cos_sin(np.arange(STEP, dtype=np.float64))


def _prepare_layer(attn_norm_w, w_in, w_alpha_fwd, b_alpha_fwd, w_alpha_bwd, b_alpha_bwd,
                   gla_norm_w, ret_norm_w, ret_norm_b, w_out, mlp_norm_w, w_ff1, w_ff2):
    o = np.cumsum((0, GLA_QK, GLA_QK, GLA_V, GLA_V, GLA_LOWRANK, GLA_LOWRANK,
                   RET_QK, RET_QK, RET_V, RET_V))
    gq, gk, gv, gg, ga_f, ga_b, rq, rk, rv, rg = (w_in[:, o[i]:o[i + 1]] for i in range(10))
    pad = jnp.zeros((D_MODEL, LANES - 2 * GLA_LOWRANK), w_in.dtype)
    w_in_p = jnp.concatenate([gq * (GLA_DK ** -0.5), gk, gv, gg, rq, rk, rv, rg, ga_f, ga_b, pad],
                             axis=1).astype(BF16)
    w_al = jnp.zeros((LANES, 2 * GLA_QK), F32)
    w_al = w_al.at[:GLA_LOWRANK, :GLA_QK].set(w_alpha_fwd)
    w_al = w_al.at[GLA_LOWRANK:2 * GLA_LOWRANK, GLA_QK:].set(w_alpha_bwd)
    b_al = jnp.concatenate([b_alpha_fwd, b_alpha_bwd])[None, :].astype(F32)
    return dict(
        anw=attn_norm_w[None, :].astype(F32), w_in=w_in_p, w_al=w_al.astype(BF16), b_al=b_al,
        gnw=gla_norm_w[None, :].astype(F32), rnw=ret_norm_w[None, :].astype(F32),
        rnb=ret_norm_b[None, :].astype(F32), w_out=w_out.astype(BF16),
        mnw=mlp_norm_w[None, :].astype(F32), w1=w_ff1.astype(BF16), w2=w_ff2.astype(BF16))


def _layer(x, lw, fnw, final_norm, interpret=False):
    batch, seq_len, _ = x.shape
    assert seq_len % STEP == 0
    ns = seq_len // STEP
    nb = seq_len // BLOCK
    dmat, df, db, ef, eb, block_decay = _retention_tables()
    rot_base, rot_step = _rotary_tables(seq_len)

    def rev(b, n):
        return (b, ns - 1 - n, 0)

    proj, cg, ug, ur = pl.pallas_call(
        functools.partial(_proj_kernel, ret_block_decay=block_decay),
        grid=(batch, ns),
        in_specs=[
            pl.BlockSpec((None, STEP, D_MODEL), rev),
            pl.BlockSpec((None, 2, LANES), lambda b, n: (ns - 1 - n, 0, 0)),
            _const_spec((2, STEP, LANES)),
            _const_spec((1, D_MODEL)),
            _const_spec((D_MODEL, W_IN_COLS)),
            _const_spec((LANES, 2 * GLA_QK)),
            _const_spec((1, 2 * GLA_QK)),
            _const_spec((BLOCK, RET_QK)),
        ],
        out_specs=[
            pl.BlockSpec((None, STEP, P_COLS), rev),
            pl.BlockSpec((None, STEP, 2 * GLA_QK), rev),
            pl.BlockSpec((None, SUBS, GLA_QK, GLA_DV), lambda b, n: (b, ns - 1 - n, 0, 0)),
            pl.BlockSpec((None, SUBS, RET_HEADS, RET_DK, RET_DV),
                         lambda b, n: (b, ns - 1 - n, 0, 0, 0)),
        ],
        out_shape=[
            jax.ShapeDtypeStruct((batch, seq_len, P_COLS), BF16),
            jax.ShapeDtypeStruct((batch, seq_len, 2 * GLA_QK), F32),
            jax.ShapeDtypeStruct((batch, nb, GLA_QK, GLA_DV), BF16),
            jax.ShapeDtypeStruct((batch, nb, RET_HEADS, RET_DK, RET_DV), BF16),
        ],
        scratch_shapes=[pltpu.VMEM((GLA_QK, GLA_DV), F32),
                        pltpu.VMEM((RET_HEADS, RET_DK, RET_DV), F32)],
        compiler_params=pltpu.CompilerParams(dimension_semantics=("arbitrary", "arbitrary"),
                                             vmem_limit_bytes=VMEM_LIMIT_BYTES),
        name="proj_bwd_state",
        interpret=interpret,
    )(x, rot_base, rot_step, lw["anw"], lw["w_in"], lw["w_al"], lw["b_al"], eb)

    total = batch * ns

    def cur(g):
        s = jnp.minimum(g, total - 1)
        return s // ns, s % ns

    def prev(g):
        s = jnp.maximum(g - 1, 0)
        return s // ns, s % ns

    return pl.pallas_call(
        functools.partial(_mix_kernel, steps_per_seq=ns, ret_block_decay=block_decay,
                          final_norm=final_norm),
        grid=(total + 1,),
        in_specs=[
            pl.BlockSpec((None, STEP, D_MODEL), lambda g: (*prev(g), 0)),
            pl.BlockSpec((None, STEP, P_COLS), lambda g: (*cur(g), 0)),
            pl.BlockSpec((None, STEP, 2 * GLA_QK), lambda g: (*cur(g), 0)),
            pl.BlockSpec((None, SUBS, GLA_QK, GLA_DV), lambda g: (*cur(g), 0, 0)),
            pl.BlockSpec((None, SUBS, RET_HEADS, RET_DK, RET_DV), lambda g: (*cur(g), 0, 0, 0)),
            _const_spec((RET_HEADS, BLOCK, BLOCK)),
            _const_spec((BLOCK, RET_QK)),
            _const_spec((BLOCK, RET_QK)),
            _const_spec((BLOCK, RET_QK)),
            _const_spec((1, GLA_DV)),
            _const_spec((1, RET_V)),
            _const_spec((1, RET_V)),
            _const_spec((D_MODEL, D_MODEL)),
            _const_spec((1, D_MODEL)),
            _const_spec((D_MODEL, D_FF)),
            _const_spec((D_FF, D_MODEL)),
            _const_spec((1, D_MODEL)),
        ],
        out_specs=pl.BlockSpec((None, STEP, D_MODEL), lambda g: (*prev(g), 0)),
        out_shape=jax.ShapeDtypeStruct((batch, seq_len, D_MODEL), F32),
        scratch_shapes=[pltpu.VMEM((GLA_QK, GLA_DV), F32),
                        pltpu.VMEM((RET_HEADS, RET_DK, RET_DV), F32),
                        pltpu.VMEM((STEP, D_MODEL), BF16)],
        compiler_params=pltpu.CompilerParams(dimension_semantics=("arbitrary",),
                                             vmem_limit_bytes=VMEM_LIMIT_BYTES),
        name="mix_mlp",
        interpret=interpret,
    )(x, proj, cg, ug, ur, dmat, df, db, ef, lw["gnw"], lw["rnw"], lw["rnb"], lw["w_out"],
      lw["mnw"], lw["w1"], lw["w2"], fnw)


def _trunk(x, layers, fnw, interpret=False):
    for i, lw in enumerate(layers):
        x = _layer(x, lw, fnw, final_norm=(i == len(layers) - 1), interpret=interpret)
    return x


def kernel(x_prompt, x_sample, attn_norm_w, w_in, w_alpha_fwd, b_alpha_fwd, w_alpha_bwd, b_alpha_bwd,
           gla_norm_w, ret_norm_w, ret_norm_b, w_out, mlp_norm_w, w_ff1, w_ff2, final_norm_w):
    stacked = (attn_norm_w, w_in, w_alpha_fwd, b_alpha_fwd, w_alpha_bwd, b_alpha_bwd,
               gla_norm_w, ret_norm_w, ret_norm_b, w_out, mlp_norm_w, w_ff1, w_ff2)
    layers = [_prepare_layer(*(w[l] for w in stacked)) for l in range(w_in.shape[0])]
    fnw = final_norm_w[None, :].astype(F32)
    return _trunk(x_prompt, layers, fnw), _trunk(x_sample, layers, fnw)
```

```python
import functools

import numpy as np
import jax
import jax.numpy as jnp
from jax import lax
from jax.experimental import pallas as pl
from jax.experimental.pallas import tpu as pltpu

F32 = jnp.float32
BF16 = jnp.bfloat16

D_MODEL = 1024
GLA_HEADS = 4
GLA_DK = 64
GLA_DV = 128
GLA_QK = GLA_HEADS * GLA_DK
GLA_V = GLA_HEADS * GLA_DV
GLA_LOWRANK = 16
GLA_NORMALIZER = 16.0
LOG_GATE_MIN = -1.0
RET_HEADS = 4
RET_DK = 128
RET_DV = 128
RET_QK = RET_HEADS * RET_DK
RET_V = RET_HEADS * RET_DV
D_FF = 4 * D_MODEL
ROPE_BASE = 10000.0
EPS = 1e-6

LANES = 128
BLOCK = 256
SUBS = 2
STEP = SUBS * BLOCK
CHUNK = 64
N_CHUNKS = BLOCK // CHUNK
FF_TILE = 512
VMEM_LIMIT_BYTES = 56 * 1024 * 1024

GQ, GK, GV, GG = 0, 256, 512, 1024
RQ, RK, RV, RG = 1536, 2048, 2560, 3072
P_COLS = 3584
GA = P_COLS
W_IN_COLS = P_COLS + LANES
STATE_ROWS = GLA_QK + RET_HEADS * RET_DK

NT = (((1,), (1,)), ((), ()))
TN = (((0,), (0,)), ((), ()))


def _dot(a, b, dims=None):
    if dims is None:
        return jnp.dot(a, b, preferred_element_type=F32)
    return lax.dot_general(a, b, dims, preferred_element_type=F32)


def _rms(x, w):
    return x * lax.rsqrt(jnp.mean(x * x, axis=-1, keepdims=True) + EPS) * w


def _silu(g):
    return g * jax.nn.sigmoid(g)


def _split3(a):
    h1 = a.astype(BF16)
    r1 = a - h1.astype(F32)
    h2 = r1.astype(BF16)
    r2 = r1 - h2.astype(F32)
    return h1, h2, r2.astype(BF16)


def _gla_state_increment(ke, v):
    blocks = []
    for pair in range(GLA_HEADS // 2):
        full = _dot(ke[:, pair * LANES:(pair + 1) * LANES],
                    v[:, pair * 2 * GLA_DV:(pair + 1) * 2 * GLA_DV], TN)
        blocks += [full[:GLA_DK, :GLA_DV], full[GLA_DK:, GLA_DV:]]
    return jnp.concatenate(blocks, axis=0)


def _as_column(row):
    return jnp.broadcast_to(row, (LANES, GLA_QK)).T


def _proj_kernel(x_ref, base_ref, rot_ref, anw_ref, win_ref, wal_ref, bal_ref, eb_ref,
                 proj_ref, cg_ref, us_ref, sg_ref, sr_ref, *, ret_block_decay):
    @pl.when(pl.program_id(1) == 0)
    def _():
        sg_ref[...] = jnp.zeros_like(sg_ref)
        sr_ref[...] = jnp.zeros_like(sr_ref)

    base = base_ref[pl.num_programs(1) - 1 - pl.program_id(1)]
    cos_b, sin_b = base[0:1, :], base[1:2, :]
    cos_i, sin_i = rot_ref[0], rot_ref[1]
    half_sign = jnp.where(lax.broadcasted_iota(jnp.int32, (1, LANES), 1) < RET_DK // 2, -1.0, 1.0)
    cos_all = cos_b * cos_i - sin_b * sin_i
    sin_all = (sin_b * cos_i + cos_b * sin_i) * half_sign

    row = lax.broadcasted_iota(jnp.int32, (CHUNK, 3 * CHUNK), 0)
    col = lax.broadcasted_iota(jnp.int32, (CHUNK, 3 * CHUNK), 1) % CHUNK
    tri_f = jnp.where(col <= row, 1.0, 0.0).astype(BF16)
    tri_b = jnp.where(col >= row, 1.0, 0.0).astype(BF16)

    def chunk_cumsums(tri, terms):
        return [_dot(tri, jnp.concatenate([t[a * CHUNK:(a + 1) * CHUNK] for t in terms], axis=0))
                for a in range(N_CHUNKS)]

    def block(sub):
        rows = slice(sub * BLOCK, (sub + 1) * BLOCK)
        h = _rms(x_ref[rows, :], anw_ref[...]).astype(BF16)

        def proj(lo, width):
            return _dot(h, win_ref[:, lo:lo + width])

        rg_ga = proj(RG, RET_V + LANES)
        proj_ref[rows, RG:RG + RET_V] = rg_ga[:, :RET_V].astype(BF16)
        ga = rg_ga[:, RET_V:].astype(BF16)
        z = _dot(ga, wal_ref[...]) + bal_ref[...]
        la = jnp.maximum(jax.nn.log_sigmoid(z) * (1.0 / GLA_NORMALIZER), LOG_GATE_MIN)
        la_f = _split3(la[:, :GLA_QK])
        la_b = _split3(la[:, GLA_QK:])

        gqk = proj(GQ, 2 * GLA_QK)
        proj_ref[rows, GQ:GQ + 2 * GLA_QK] = gqk.astype(BF16)
        gk = gqk[:, GLA_QK:]
        gv = proj(GV, GLA_V).astype(BF16)
        proj_ref[rows, GV:GV + GLA_V] = gv
        proj_ref[rows, GG:GG + GLA_V] = proj(GG, GLA_V).astype(BF16)
        rq_all = proj(RQ, RET_QK)
        rk_all = proj(RK, RET_QK)
        rv_all = proj(RV, RET_V).astype(BF16)
        proj_ref[rows, RV:RV + RET_V] = rv_all

        cum_b = chunk_cumsums(tri_b, la_b)
        cg_ref[rows, :GLA_QK] = jnp.concatenate(chunk_cumsums(tri_f, la_f), axis=0)
        cg_ref[rows, GLA_QK:] = jnp.concatenate(cum_b, axis=0)
        offset = jnp.zeros((1, GLA_QK), F32)
        for a in reversed(range(N_CHUNKS)):
            cum_b[a], offset = cum_b[a] + offset, offset + cum_b[a][0:1, :]
        cb = jnp.concatenate(cum_b, axis=0)

        cos = cos_all[rows, :]
        sin = sin_all[rows, :]
        for hh in range(RET_HEADS):
            lo = hh * RET_DK
            us_ref[sub, GLA_QK + lo:GLA_QK + lo + RET_DK, :] = sr_ref[hh].astype(BF16)
            q = rq_all[:, lo:lo + RET_DK]
            k = rk_all[:, lo:lo + RET_DK]
            q = q * cos + pltpu.roll(q, RET_DK // 2, axis=1) * sin
            k = k * cos + pltpu.roll(k, RET_DK // 2, axis=1) * sin
            proj_ref[rows, RQ + lo:RQ + lo + RET_DK] = q.astype(BF16)
            proj_ref[rows, RK + lo:RK + lo + RET_DK] = k.astype(BF16)
            v = rv_all[:, lo:lo + RET_DV]
            kd = (k * eb_ref[:, lo:lo + RET_DK]).astype(BF16)
            sr_ref[hh] = sr_ref[hh] * ret_block_decay[hh] + _dot(kd, v, TN)

        us_ref[sub, :GLA_QK, :] = sg_ref[...].astype(BF16)
        ctot = cb[0:1, :]
        kb = (gk * jnp.exp(ctot - cb)).astype(BF16)
        sg_ref[...] = sg_ref[...] * _as_column(jnp.exp(ctot)) + _gla_state_increment(kb, gv)

    for sub in reversed(range(SUBS)):
        block(sub)


def _mix_kernel(xprev_ref, proj_ref, cg_ref, us_ref, dmat_ref, df_ref, db_ref, ef_ref,
                gnw_ref, rnw_ref, rnb_ref, wout_ref, mnw_ref, w1_ref, w2_ref, fnw_ref,
                y_ref, fg_ref, fr_ref, mixed_ref, ff_ref, *, steps_per_seq, ret_block_decay,
                final_norm):
    g = pl.program_id(0)

    @pl.when(g % steps_per_seq == 0)
    def _():
        fg_ref[...] = jnp.zeros_like(fg_ref)
        fr_ref[...] = jnp.zeros_like(fr_ref)

    @pl.when(g == 0)
    def _():
        mixed_ref[...] = jnp.zeros_like(mixed_ref)

    dense = {}

    def dense_head():
        x1 = xprev_ref[...] + _dot(mixed_ref[...], wout_ref[...])
        dense["h2"] = _rms(x1, mnw_ref[...]).astype(BF16)
        dense["acc"] = x1

    def dense_up(f):
        ff = jnp.maximum(_dot(dense["h2"], w1_ref[:, f * FF_TILE:(f + 1) * FF_TILE]), 0.0)
        ff_ref[:, f * FF_TILE:(f + 1) * FF_TILE] = (ff * ff).astype(BF16)

    def dense_down(half):
        cols = slice(half * (D_FF // 2), (half + 1) * (D_FF // 2))
        dense["acc"] = dense["acc"] + _dot(ff_ref[:, cols], w2_ref[cols, :])

    def dense_finish():
        y_ref[...] = _rms(dense["acc"], fnw_ref[...]) if final_norm else dense["acc"]

    n_ff = D_FF // FF_TILE
    dense_head()

    def gla_operands(sub, a, forward):
        r0 = sub * BLOCK + a * CHUNK
        cum = cg_ref[r0:r0 + CHUNK, 0:GLA_QK] if forward else cg_ref[r0:r0 + CHUNK, GLA_QK:2 * GLA_QK]
        e = jnp.exp(cum)
        e_edge = e[CHUNK - 1:CHUNK, :] if forward else e[0:1, :]
        q = proj_ref[r0:r0 + CHUNK, GQ:GQ + GLA_QK].astype(F32)
        k = proj_ref[r0:r0 + CHUNK, GK:GK + GLA_QK].astype(F32)
        ki = k * jnp.exp(-cum)
        return dict(qd=(q * e).astype(BF16), ki=ki, ke=(ki * e_edge).astype(BF16),
                    e_edge=e_edge, v=proj_ref[r0:r0 + CHUNK, GV:GV + GLA_V])

    sweeps = {True: list(range(N_CHUNKS)), False: list(reversed(range(N_CHUNKS)))}
    ops = {(sub, fw, a): gla_operands(sub, a, fw)
           for sub in range(SUBS) for fw in (True, False) for a in sweeps[fw]}

    ret = {}
    for sub in range(SUBS):
        rows = slice(sub * BLOCK, (sub + 1) * BLOCK)
        for h in range(RET_HEADS):
            lo = h * RET_DK
            q = proj_ref[rows, RQ + lo:RQ + lo + RET_DK]
            k = proj_ref[rows, RK + lo:RK + lo + RET_DK]
            qf = q.astype(F32)
            ret[(sub, h)] = dict(
                q=q, k=k, v=proj_ref[rows, RV + lo:RV + lo + RET_DV],
                q_both=jnp.concatenate([(qf * df_ref[:, lo:lo + RET_DK]).astype(BF16),
                                        (qf * db_ref[:, lo:lo + RET_DK]).astype(BF16)], axis=1),
                kd=(k.astype(F32) * ef_ref[:, lo:lo + RET_DK]).astype(BF16))

    n_pairs = GLA_HEADS // 2
    lane = lax.broadcasted_iota(jnp.int32, (CHUNK, LANES), 1)
    even_lanes = lane < GLA_DK
    for key, op in ops.items():
        _, fw, a = key
        if fw or a > 0:
            op["inc"] = _gla_state_increment(op["ke"], op["v"])
        op["s"] = []
        for pair in range(n_pairs):
            ki = op["ki"][:, pair * LANES:(pair + 1) * LANES]
            keys = jnp.concatenate([jnp.where(even_lanes, 0.0, ki), jnp.where(even_lanes, ki, 0.0)],
                                   axis=0)
            op["s"].append(_dot(op["qd"][:, pair * LANES:(pair + 1) * LANES], keys.T.astype(BF16)))
    for r in ret.values():
        r["inc"] = _dot(r["kd"], r["v"], TN)
        r["s"] = _dot(r["q"], r["k"], NT)
    for h in range(RET_HEADS):
        state = fr_ref[h]
        for sub in range(SUBS):
            r = ret[(sub, h)]
            entering = us_ref[sub, GLA_QK + h * RET_DK:GLA_QK + (h + 1) * RET_DK, :]
            states = jnp.concatenate([state.astype(BF16), entering], axis=0)
            r["inter"] = _dot(r["q_both"], states)
            state = state * ret_block_decay[h] + r["inc"]
        fr_ref[h] = state

    for f in range(n_ff):
        dense_up(f)

    ri = lax.broadcasted_iota(jnp.int32, (CHUNK, LANES), 0)
    ci = lane % CHUNK
    state = fg_ref[...]
    for sub in range(SUBS):
        for fw in (True, False):
            if not fw:
                fwd_state, state = state, us_ref[sub, :GLA_QK, :].astype(F32)
            keep = (ci <= ri) if fw else (ci > ri)
            for a in sweeps[fw]:
                op = ops[(sub, fw, a)]
                op["state"] = state.astype(BF16)
                op["p"] = [jnp.where(keep, s, 0.0).astype(BF16) for s in op["s"]]
                if "inc" in op:
                    state = state * _as_column(op["e_edge"]) + op["inc"]
        state = fwd_state
    fg_ref[...] = state
    for (sub, h), r in ret.items():
        r["p"] = (r["s"] * dmat_ref[h]).astype(BF16)

    for op in ops.values():
        outs = []
        for h in range(GLA_HEADS):
            pair, odd = divmod(h, 2)
            qd = op["qd"][:, pair * LANES:(pair + 1) * LANES]
            p = op["p"][pair]
            v = op["v"][:, h * GLA_DV:(h + 1) * GLA_DV]
            s_h = op["state"][h * GLA_DK:(h + 1) * GLA_DK, :]
            if odd:
                lhs = jnp.where(even_lanes, p, qd)
                rhs = jnp.concatenate([v, s_h], axis=0)
            else:
                lhs = jnp.where(even_lanes, qd, p)
                rhs = jnp.concatenate([s_h, v], axis=0)
            outs.append(_dot(lhs, rhs))
        op["o"] = jnp.concatenate(outs, axis=1)
    for r in ret.values():
        r["o"] = _dot(r["p"], r["v"]) + r["inter"]

    for half in range(2):
        dense_down(half)

    gnw = gnw_ref[...]
    for sub in range(SUBS):
        rows = slice(sub * BLOCK, (sub + 1) * BLOCK)
        o_gla = jnp.concatenate([ops[(sub, True, a)]["o"] + ops[(sub, False, a)]["o"]
                                 for a in range(N_CHUNKS)], axis=0)
        mixed = []
        for h in range(GLA_HEADS):
            lo = h * GLA_DV
            oh = _rms(o_gla[:, lo:lo + GLA_DV], gnw)
            mixed.append(oh * _silu(proj_ref[rows, GG + lo:GG + lo + GLA_DV].astype(F32)))
        for h in range(RET_HEADS):
            lo = h * RET_DV
            o = ret[(sub, h)]["o"]
            mu = jnp.mean(o, axis=-1, keepdims=True)
            oc = o - mu
            var = jnp.mean(oc * oc, axis=-1, keepdims=True)
            on = oc * lax.rsqrt(var + EPS) * rnw_ref[:, lo:lo + RET_DV] + rnb_ref[:, lo:lo + RET_DV]
            mixed.append(on * _silu(proj_ref[rows, RG + lo:RG + lo + RET_DV].astype(F32)))
        mixed_ref[rows, :] = jnp.concatenate(mixed, axis=1).astype(BF16)
    dense_finish()


def _const_spec(shape):
    zeros = (0,) * len(shape)
    return pl.BlockSpec(shape, lambda *_: zeros, pipeline_mode=pl.Buffered(1))


def _retention_tables():
    log_gamma = np.log1p(-np.power(2.0, -5.0 - np.arange(RET_HEADS, dtype=np.float64)))
    i = np.arange(BLOCK, dtype=np.float64)
    scale = RET_DK ** -0.5
    dist = np.abs(i[:, None] - i[None, :])
    dmat = np.exp(log_gamma[:, None, None] * dist[None]) * scale

    def expand(t):
        return np.repeat(t, RET_DK, axis=1).astype(np.float32)

    df = expand(np.exp(np.outer(i + 1.0, log_gamma)) * scale)
    db = expand(np.exp(np.outer(BLOCK - i, log_gamma)) * scale)
    ef = expand(np.exp(np.outer(BLOCK - 1.0 - i, log_gamma)))
    eb = expand(np.exp(np.outer(i, log_gamma)))
    block_decay = tuple(float(g) for g in np.exp(BLOCK * log_gamma))
    return dmat.astype(np.float32), df, db, ef, eb, block_decay


def _rotary_tables(seq_len):
    half = RET_DK // 2
    inv_freq = np.power(ROPE_BASE, -np.arange(half, dtype=np.float64) / half)

    def cos_sin(pos):
        ang = pos[:, None] * inv_freq[None, :]
        return np.stack([np.tile(np.cos(ang), (1, 2)), np.tile(np.sin(ang), (1, 2))]).astype(np.float32)

    base = cos_sin(np.arange(0, seq_len, STEP, dtype=np.float64))
    return np.ascontiguousarray(base.transpose(1, 0, 2)), cos_sin(np.arange(STEP, dtype=np.float64))


def _prepare_layer(attn_norm_w, w_in, w_alpha_fwd, b_alpha_fwd, w_alpha_bwd, b_alpha_bwd,
                   gla_norm_w, ret_norm_w, ret_norm_b, w_out, mlp_norm_w, w_ff1, w_ff2):
    o = np.cumsum((0, GLA_QK, GLA_QK, GLA_V, GLA_V, GLA_LOWRANK, GLA_LOWRANK,
                   RET_QK, RET_QK, RET_V, RET_V))
    gq, gk, gv, gg, ga_f, ga_b, rq, rk, rv, rg = (w_in[:, o[i]:o[i + 1]] for i in range(10))
    pad = jnp.zeros((D_MODEL, LANES - 2 * GLA_LOWRANK), w_in.dtype)
    w_in_p = jnp.concatenate([gq * (GLA_DK ** -0.5), gk, gv, gg, rq, rk, rv, rg, ga_f, ga_b, pad],
                             axis=1).astype(BF16)
    w_al = jnp.zeros((LANES, 2 * GLA_QK), F32)
    w_al = w_al.at[:GLA_LOWRANK, :GLA_QK].set(w_alpha_fwd)
    w_al = w_al.at[GLA_LOWRANK:2 * GLA_LOWRANK, GLA_QK:].set(w_alpha_bwd)
    b_al = jnp.concatenate([b_alpha_fwd, b_alpha_bwd])[None, :].astype(F32)
    return dict(
        anw=attn_norm_w[None, :].astype(F32), w_in=w_in_p, w_al=w_al.astype(BF16), b_al=b_al,
        gnw=gla_norm_w[None, :].astype(F32), rnw=ret_norm_w[None, :].astype(F32),
        rnb=ret_norm_b[None, :].astype(F32), w_out=w_out.astype(BF16),
        mnw=mlp_norm_w[None, :].astype(F32), w1=w_ff1.astype(BF16), w2=w_ff2.astype(BF16))


def _layer(x, lw, fnw, final_norm, interpret=False):
    batch, seq_len, _ = x.shape
    assert seq_len % STEP == 0
    ns = seq_len // STEP
    nb = seq_len // BLOCK
    dmat, df, db, ef, eb, block_decay = _retention_tables()
    rot_base, rot_step = _rotary_tables(seq_len)

    def rev(b, n):
        return (b, ns - 1 - n, 0)

    proj, cg, us = pl.pallas_call(
        functools.partial(_proj_kernel, ret_block_decay=block_decay),
        grid=(batch, ns),
        in_specs=[
            pl.BlockSpec((None, STEP, D_MODEL), rev),
            _const_spec((ns, 2, LANES)),
            _const_spec((2, STEP, LANES)),
            _const_spec((1, D_MODEL)),
            _const_spec((D_MODEL, W_IN_COLS)),
            _const_spec((LANES, 2 * GLA_QK)),
            _const_spec((1, 2 * GLA_QK)),
            _const_spec((BLOCK, RET_QK)),
        ],
        out_specs=[
            pl.BlockSpec((None, STEP, P_COLS), rev),
            pl.BlockSpec((None, STEP, 2 * GLA_QK), rev),
            pl.BlockSpec((None, SUBS, STATE_ROWS, LANES), lambda b, n: (b, ns - 1 - n, 0, 0)),
        ],
        out_shape=[
            jax.ShapeDtypeStruct((batch, seq_len, P_COLS), BF16),
            jax.ShapeDtypeStruct((batch, seq_len, 2 * GLA_QK), F32),
            jax.ShapeDtypeStruct((batch, nb, STATE_ROWS, LANES), BF16),
        ],
        scratch_shapes=[pltpu.VMEM((GLA_QK, GLA_DV), F32),
                        pltpu.VMEM((RET_HEADS, RET_DK, RET_DV), F32)],
        compiler_params=pltpu.CompilerParams(dimension_semantics=("arbitrary", "arbitrary"),
                                             vmem_limit_bytes=VMEM_LIMIT_BYTES),
        name="proj_bwd_state",
        interpret=interpret,
    )(x, rot_base, rot_step, lw["anw"], lw["w_in"], lw["w_al"], lw["b_al"], eb)

    total = batch * ns

    def cur(g):
        s = jnp.minimum(g, total - 1)
        return s // ns, s % ns

    def prev(g):
        s = jnp.maximum(g - 1, 0)
        return s // ns, s % ns

    return pl.pallas_call(
        functools.partial(_mix_kernel, steps_per_seq=ns, ret_block_decay=block_decay,
                          final_norm=final_norm),
        grid=(total + 1,),
        in_specs=[
            pl.BlockSpec((None, STEP, D_MODEL), lambda g: (*prev(g), 0)),
            pl.BlockSpec((None, STEP, P_COLS), lambda g: (*cur(g), 0)),
            pl.BlockSpec((None, STEP, 2 * GLA_QK), lambda g: (*cur(g), 0)),
            pl.BlockSpec((None, SUBS, STATE_ROWS, LANES), lambda g: (*cur(g), 0, 0)),
            _const_spec((RET_HEADS, BLOCK, BLOCK)),
            _const_spec((BLOCK, RET_QK)),
            _const_spec((BLOCK, RET_QK)),
            _const_spec((BLOCK, RET_QK)),
            _const_spec((1, GLA_DV)),
            _const_spec((1, RET_V)),
            _const_spec((1, RET_V)),
            _const_spec((D_MODEL, D_MODEL)),
            _const_spec((1, D_MODEL)),
            _const_spec((D_MODEL, D_FF)),
            _const_spec((D_FF, D_MODEL)),
            _const_spec((1, D_MODEL)),
        ],
        out_specs=pl.BlockSpec((None, STEP, D_MODEL), lambda g: (*prev(g), 0)),
        out_shape=jax.ShapeDtypeStruct((batch, seq_len, D_MODEL), F32),
        scratch_shapes=[pltpu.VMEM((GLA_QK, GLA_DV), F32),
                        pltpu.VMEM((RET_HEADS, RET_DK, RET_DV), F32),
                        pltpu.VMEM((STEP, D_MODEL), BF16),
                        pltpu.VMEM((STEP, D_FF), BF16)],
        compiler_params=pltpu.CompilerParams(dimension_semantics=("arbitrary",),
                                             vmem_limit_bytes=VMEM_LIMIT_BYTES),
        name="mix_mlp",
        interpret=interpret,
    )(x, proj, cg, us, dmat, df, db, ef, lw["gnw"], lw["rnw"], lw["rnb"], lw["w_out"],
      lw["mnw"], lw["w1"], lw["w2"], fnw)


def _trunk(x, layers, fnw, interpret=False):
    for i, lw in enumerate(layers):
        x = _layer(x, lw, fnw, final_norm=(i == len(layers) - 1), interpret=interpret)
    return x


def kernel(x_prompt, x_sample, attn_norm_w, w_in, w_alpha_fwd, b_alpha_fwd, w_alpha_bwd, b_alpha_bwd,
           gla_norm_w, ret_norm_w, ret_norm_b, w_out, mlp_norm_w, w_ff1, w_ff2, final_norm_w):
    stacked = (attn_norm_w, w_in, w_alpha_fwd, b_alpha_fwd, w_alpha_bwd, b_alpha_bwd,
               gla_norm_w, ret_norm_w, ret_norm_b, w_out, mlp_norm_w, w_ff1, w_ff2)
    layers = [_prepare_layer(*(w[l] for w in stacked)) for l in range(w_in.shape[0])]
    fnw = final_norm_w[None, :].astype(F32)
    return _trunk(x_prompt, layers, fnw), _trunk(x_sample, layers, fnw)
```

```python
import functools

import numpy as np
import jax
import jax.numpy as jnp
from jax import lax
from jax.experimental import pallas as pl
from jax.experimental.pallas import tpu as pltpu

F32 = jnp.float32
BF16 = jnp.bfloat16

D_MODEL = 1024
GLA_HEADS = 4
GLA_DK = 64
GLA_DV = 128
GLA_QK = GLA_HEADS * GLA_DK
GLA_V = GLA_HEADS * GLA_DV
GLA_LOWRANK = 16
GLA_NORMALIZER = 16.0
LOG_GATE_MIN = -1.0
RET_HEADS = 4
RET_DK = 128
RET_DV = 128
RET_QK = RET_HEADS * RET_DK
RET_V = RET_HEADS * RET_DV
D_FF = 4 * D_MODEL
ROPE_BASE = 10000.0
EPS = 1e-6

LANES = 128
BLOCK = 256
SUBS = 2
STEP = SUBS * BLOCK
PROJ_SUBS = 4
PROJ_STEP = PROJ_SUBS * BLOCK
CHUNK = 64
N_CHUNKS = BLOCK // CHUNK
FF_TILE = 512
VMEM_LIMIT_BYTES = 56 * 1024 * 1024

GQ, GK, GV, GG = 0, 256, 512, 1024
RQ, RK, RV, RG = 1536, 2048, 2560, 3072
P_COLS = 3584
GA = P_COLS
W_IN_COLS = P_COLS + LANES
STATE_ROWS = GLA_QK + RET_HEADS * RET_DK

NT = (((1,), (1,)), ((), ()))
TN = (((0,), (0,)), ((), ()))


def _dot(a, b, dims=None):
    if dims is None:
        return jnp.dot(a, b, preferred_element_type=F32)
    return lax.dot_general(a, b, dims, preferred_element_type=F32)


def _rms(x, w):
    return x * lax.rsqrt(jnp.mean(x * x, axis=-1, keepdims=True) + EPS) * w


def _silu(g):
    return g * jax.nn.sigmoid(g)


def _split3(a):
    h1 = a.astype(BF16)
    r1 = a - h1.astype(F32)
    h2 = r1.astype(BF16)
    r2 = r1 - h2.astype(F32)
    return h1, h2, r2.astype(BF16)


def _gla_state_increment(ke, v):
    blocks = []
    for pair in range(GLA_HEADS // 2):
        full = _dot(ke[:, pair * LANES:(pair + 1) * LANES],
                    v[:, pair * 2 * GLA_DV:(pair + 1) * 2 * GLA_DV], TN)
        blocks += [full[:GLA_DK, :GLA_DV], full[GLA_DK:, GLA_DV:]]
    return jnp.concatenate(blocks, axis=0)


def _as_column(row):
    return jnp.broadcast_to(row, (LANES, GLA_QK)).T


def _proj_kernel(x_ref, base_ref, rot_ref, anw_ref, win_ref, wal_ref, bal_ref, eb_ref,
                 proj_ref, cg_ref, us_ref, sg_ref, sr_ref, *, ret_block_decay):
    @pl.when(pl.program_id(1) == 0)
    def _():
        sg_ref[...] = jnp.zeros_like(sg_ref)
        sr_ref[...] = jnp.zeros_like(sr_ref)

    base = base_ref[pl.num_programs(1) - 1 - pl.program_id(1)]
    cos_b, sin_b = base[0:1, :], base[1:2, :]
    cos_i, sin_i = rot_ref[0], rot_ref[1]
    half_sign = jnp.where(lax.broadcasted_iota(jnp.int32, (1, LANES), 1) < RET_DK // 2, -1.0, 1.0)
    cos_all = cos_b * cos_i - sin_b * sin_i
    sin_all = (sin_b * cos_i + cos_b * sin_i) * half_sign

    row = lax.broadcasted_iota(jnp.int32, (CHUNK, 3 * CHUNK), 0)
    col = lax.broadcasted_iota(jnp.int32, (CHUNK, 3 * CHUNK), 1) % CHUNK
    tri_f = jnp.where(col <= row, 1.0, 0.0).astype(BF16)
    tri_b = jnp.where(col >= row, 1.0, 0.0).astype(BF16)

    def chunk_cumsums(tri, terms):
        return [_dot(tri, jnp.concatenate([t[a * CHUNK:(a + 1) * CHUNK] for t in terms], axis=0))
                for a in range(N_CHUNKS)]

    def block(sub):
        rows = slice(sub * BLOCK, (sub + 1) * BLOCK)
        h = _rms(x_ref[rows, :], anw_ref[...]).astype(BF16)

        def proj(lo, width):
            return _dot(h, win_ref[:, lo:lo + width])

        rg_ga = proj(RG, RET_V + LANES)
        proj_ref[rows, RG:RG + RET_V] = rg_ga[:, :RET_V].astype(BF16)
        ga = rg_ga[:, RET_V:].astype(BF16)
        z = _dot(ga, wal_ref[...]) + bal_ref[...]
        la = jnp.maximum(jax.nn.log_sigmoid(z) * (1.0 / GLA_NORMALIZER), LOG_GATE_MIN)
        la_f = _split3(la[:, :GLA_QK])
        la_b = _split3(la[:, GLA_QK:])

        gqk = proj(GQ, 2 * GLA_QK)
        proj_ref[rows, GQ:GQ + 2 * GLA_QK] = gqk.astype(BF16)
        gk = gqk[:, GLA_QK:]
        gv = proj(GV, GLA_V).astype(BF16)
        proj_ref[rows, GV:GV + GLA_V] = gv
        proj_ref[rows, GG:GG + GLA_V] = proj(GG, GLA_V).astype(BF16)
        rq_all = proj(RQ, RET_QK)
        rk_all = proj(RK, RET_QK)
        rv_all = proj(RV, RET_V).astype(BF16)
        proj_ref[rows, RV:RV + RET_V] = rv_all

        cum_b = chunk_cumsums(tri_b, la_b)
        cg_ref[rows, :GLA_QK] = jnp.concatenate(chunk_cumsums(tri_f, la_f), axis=0)
        cg_ref[rows, GLA_QK:] = jnp.concatenate(cum_b, axis=0)
        offset = jnp.zeros((1, GLA_QK), F32)
        for a in reversed(range(N_CHUNKS)):
            cum_b[a], offset = cum_b[a] + offset, offset + cum_b[a][0:1, :]
        cb = jnp.concatenate(cum_b, axis=0)

        cos = cos_all[rows, :]
        sin = sin_all[rows, :]
        for hh in range(RET_HEADS):
            lo = hh * RET_DK
            us_ref[sub, GLA_QK + lo:GLA_QK + lo + RET_DK, :] = sr_ref[hh].astype(BF16)
            q = rq_all[:, lo:lo + RET_DK]
            k = rk_all[:, lo:lo + RET_DK]
            q = q * cos + pltpu.roll(q, RET_DK // 2, axis=1) * sin
            k = k * cos + pltpu.roll(k, RET_DK // 2, axis=1) * sin
            proj_ref[rows, RQ + lo:RQ + lo + RET_DK] = q.astype(BF16)
            proj_ref[rows, RK + lo:RK + lo + RET_DK] = k.astype(BF16)
            v = rv_all[:, lo:lo + RET_DV]
            kd = (k * eb_ref[:, lo:lo + RET_DK]).astype(BF16)
            sr_ref[hh] = sr_ref[hh] * ret_block_decay[hh] + _dot(kd, v, TN)

        us_ref[sub, :GLA_QK, :] = sg_ref[...].astype(BF16)
        ctot = cb[0:1, :]
        kb = (gk * jnp.exp(ctot - cb)).astype(BF16)
        sg_ref[...] = sg_ref[...] * _as_column(jnp.exp(ctot)) + _gla_state_increment(kb, gv)

    for sub in reversed(range(PROJ_SUBS)):
        block(sub)


def _mix_kernel(xprev_ref, proj_ref, cg_ref, us_ref, dmat_ref, df_ref, db_ref, ef_ref,
                gnw_ref, rnw_ref, rnb_ref, wout_ref, mnw_ref, w1_ref, w2_ref, fnw_ref,
                y_ref, fg_ref, fr_ref, mixed_ref, ff_ref, *, steps_per_seq, ret_block_decay,
                final_norm):
    g = pl.program_id(0)

    @pl.when(g % steps_per_seq == 0)
    def _():
        fg_ref[...] = jnp.zeros_like(fg_ref)
        fr_ref[...] = jnp.zeros_like(fr_ref)

    @pl.when(g == 0)
    def _():
        mixed_ref[...] = jnp.zeros_like(mixed_ref)

    dense = {}

    def dense_head():
        x1 = xprev_ref[...] + _dot(mixed_ref[...], wout_ref[...])
        dense["h2"] = _rms(x1, mnw_ref[...]).astype(BF16)
        dense["acc"] = x1

    def dense_up(f):
        ff = jnp.maximum(_dot(dense["h2"], w1_ref[:, f * FF_TILE:(f + 1) * FF_TILE]), 0.0)
        ff_ref[:, f * FF_TILE:(f + 1) * FF_TILE] = (ff * ff).astype(BF16)

    def dense_down(half):
        cols = slice(half * (D_FF // 2), (half + 1) * (D_FF // 2))
        dense["acc"] = dense["acc"] + _dot(ff_ref[:, cols], w2_ref[cols, :])

    def dense_finish():
        y_ref[...] = _rms(dense["acc"], fnw_ref[...]) if final_norm else dense["acc"]

    n_ff = D_FF // FF_TILE
    dense_head()

    def gla_operands(sub, a, forward):
        r0 = sub * BLOCK + a * CHUNK
        cum = cg_ref[r0:r0 + CHUNK, 0:GLA_QK] if forward else cg_ref[r0:r0 + CHUNK, GLA_QK:2 * GLA_QK]
        e = jnp.exp(cum)
        e_edge = e[CHUNK - 1:CHUNK, :] if forward else e[0:1, :]
        q = proj_ref[r0:r0 + CHUNK, GQ:GQ + GLA_QK].astype(F32)
        k = proj_ref[r0:r0 + CHUNK, GK:GK + GLA_QK].astype(F32)
        ki = k * jnp.exp(-cum)
        return dict(qd=(q * e).astype(BF16), ki=ki, ke=(ki * e_edge).astype(BF16),
                    e_edge=e_edge, v=proj_ref[r0:r0 + CHUNK, GV:GV + GLA_V])

    sweeps = {True: list(range(N_CHUNKS)), False: list(reversed(range(N_CHUNKS)))}
    ops = {(sub, fw, a): gla_operands(sub, a, fw)
           for sub in range(SUBS) for fw in (True, False) for a in sweeps[fw]}

    ret = {}
    for sub in range(SUBS):
        rows = slice(sub * BLOCK, (sub + 1) * BLOCK)
        for h in range(RET_HEADS):
            lo = h * RET_DK
            q = proj_ref[rows, RQ + lo:RQ + lo + RET_DK]
            k = proj_ref[rows, RK + lo:RK + lo + RET_DK]
            qf = q.astype(F32)
            ret[(sub, h)] = dict(
                q=q, k=k, v=proj_ref[rows, RV + lo:RV + lo + RET_DV],
                q_both=jnp.concatenate([(qf * df_ref[:, lo:lo + RET_DK]).astype(BF16),
                                        (qf * db_ref[:, lo:lo + RET_DK]).astype(BF16)], axis=1),
                kd=(k.astype(F32) * ef_ref[:, lo:lo + RET_DK]).astype(BF16))

    n_pairs = GLA_HEADS // 2
    lane = lax.broadcasted_iota(jnp.int32, (CHUNK, LANES), 1)
    even_lanes = lane < GLA_DK
    for key, op in ops.items():
        _, fw, a = key
        if fw or a > 0:
            op["inc"] = _gla_state_increment(op["ke"], op["v"])
        op["s"] = []
        for pair in range(n_pairs):
            ki = op["ki"][:, pair * LANES:(pair + 1) * LANES]
            keys = jnp.concatenate([jnp.where(even_lanes, 0.0, ki), jnp.where(even_lanes, ki, 0.0)],
                                   axis=0)
            op["s"].append(_dot(op["qd"][:, pair * LANES:(pair + 1) * LANES], keys.T.astype(BF16)))
    for r in ret.values():
        r["inc"] = _dot(r["kd"], r["v"], TN)
        r["s"] = _dot(r["q"], r["k"], NT)
    for h in range(RET_HEADS):
        state = fr_ref[h]
        for sub in range(SUBS):
            r = ret[(sub, h)]
            entering = us_ref[sub, GLA_QK + h * RET_DK:GLA_QK + (h + 1) * RET_DK, :]
            states = jnp.concatenate([state.astype(BF16), entering], axis=0)
            r["inter"] = _dot(r["q_both"], states)
            state = state * ret_block_decay[h] + r["inc"]
        fr_ref[h] = state

    for f in range(n_ff):
        dense_up(f)

    ri = lax.broadcasted_iota(jnp.int32, (CHUNK, LANES), 0)
    ci = lane % CHUNK
    state = fg_ref[...]
    for sub in range(SUBS):
        for fw in (True, False):
            if not fw:
                fwd_state, state = state, us_ref[sub, :GLA_QK, :].astype(F32)
            keep = (ci <= ri) if fw else (ci > ri)
            for a in sweeps[fw]:
                op = ops[(sub, fw, a)]
                op["state"] = state.astype(BF16)
                op["p"] = [jnp.where(keep, s, 0.0).astype(BF16) for s in op["s"]]
                if "inc" in op:
                    state = state * _as_column(op["e_edge"]) + op["inc"]
        state = fwd_state
    fg_ref[...] = state
    for (sub, h), r in ret.items():
        r["p"] = (r["s"] * dmat_ref[h]).astype(BF16)

    for op in ops.values():
        outs = []
        for h in range(GLA_HEADS):
            pair, odd = divmod(h, 2)
            qd = op["qd"][:, pair * LANES:(pair + 1) * LANES]
            p = op["p"][pair]
            v = op["v"][:, h * GLA_DV:(h + 1) * GLA_DV]
            s_h = op["state"][h * GLA_DK:(h + 1) * GLA_DK, :]
            if odd:
                lhs = jnp.where(even_lanes, p, qd)
                rhs = jnp.concatenate([v, s_h], axis=0)
            else:
                lhs = jnp.where(even_lanes, qd, p)
                rhs = jnp.concatenate([s_h, v], axis=0)
            outs.append(_dot(lhs, rhs))
        op["o"] = jnp.concatenate(outs, axis=1)
    for r in ret.values():
        r["o"] = _dot(r["p"], r["v"]) + r["inter"]

    for half in range(2):
        dense_down(half)

    gnw = gnw_ref[...]
    for sub in range(SUBS):
        rows = slice(sub * BLOCK, (sub + 1) * BLOCK)
        o_gla = jnp.concatenate([ops[(sub, True, a)]["o"] + ops[(sub, False, a)]["o"]
                                 for a in range(N_CHUNKS)], axis=0)
        mixed = []
        for h in range(GLA_HEADS):
            lo = h * GLA_DV
            oh = _rms(o_gla[:, lo:lo + GLA_DV], gnw)
            mixed.append(oh * _silu(proj_ref[rows, GG + lo:GG + lo + GLA_DV].astype(F32)))
        for h in range(RET_HEADS):
            lo = h * RET_DV
            o = ret[(sub, h)]["o"]
            mu = jnp.mean(o, axis=-1, keepdims=True)
            oc = o - mu
            var = jnp.mean(oc * oc, axis=-1, keepdims=True)
            on = oc * lax.rsqrt(var + EPS) * rnw_ref[:, lo:lo + RET_DV] + rnb_ref[:, lo:lo + RET_DV]
            mixed.append(on * _silu(proj_ref[rows, RG + lo:RG + lo + RET_DV].astype(F32)))
        mixed_ref[rows, :] = jnp.concatenate(mixed, axis=1).astype(BF16)
    dense_finish()


def _const_spec(shape):
    zeros = (0,) * len(shape)
    return pl.BlockSpec(shape, lambda *_: zeros, pipeline_mode=pl.Buffered(1))


def _retention_tables():
    log_gamma = np.log1p(-np.power(2.0, -5.0 - np.arange(RET_HEADS, dtype=np.float64)))
    i = np.arange(BLOCK, dtype=np.float64)
    scale = RET_DK ** -0.5
    dist = np.abs(i[:, None] - i[None, :])
    dmat = np.exp(log_gamma[:, None, None] * dist[None]) * scale

    def expand(t):
        return np.repeat(t, RET_DK, axis=1).astype(np.float32)

    df = expand(np.exp(np.outer(i + 1.0, log_gamma)) * scale)
    db = expand(np.exp(np.outer(BLOCK - i, log_gamma)) * scale)
    ef = expand(np.exp(np.outer(BLOCK - 1.0 - i, log_gamma)))
    eb = expand(np.exp(np.outer(i, log_gamma)))
    block_decay = tuple(float(g) for g in np.exp(BLOCK * log_gamma))
    return dmat.astype(np.float32), df, db, ef, eb, block_decay


def _rotary_tables(seq_len):
    half = RET_DK // 2
    inv_freq = np.power(ROPE_BASE, -np.arange(half, dtype=np.float64) / half)

    def cos_sin(pos):
        ang = pos[:, None] * inv_freq[None, :]
        return np.stack([np.tile(np.cos(ang), (1, 2)), np.tile(np.sin(ang), (1, 2))]).astype(np.float32)

    base = cos_sin(np.arange(0, seq_len, PROJ_STEP, dtype=np.float64))
    return (np.ascontiguousarray(base.transpose(1, 0, 2)),
            cos_sin(np.arange(PROJ_STEP, dtype=np.float64)))


def _prepare_layer(attn_norm_w, w_in, w_alpha_fwd, b_alpha_fwd, w_alpha_bwd, b_alpha_bwd,
                   gla_norm_w, ret_norm_w, ret_norm_b, w_out, mlp_norm_w, w_ff1, w_ff2):
    o = np.cumsum((0, GLA_QK, GLA_QK, GLA_V, GLA_V, GLA_LOWRANK, GLA_LOWRANK,
                   RET_QK, RET_QK, RET_V, RET_V))
    gq, gk, gv, gg, ga_f, ga_b, rq, rk, rv, rg = (w_in[:, o[i]:o[i + 1]] for i in range(10))
    pad = jnp.zeros((D_MODEL, LANES - 2 * GLA_LOWRANK), w_in.dtype)
    w_in_p = jnp.concatenate([gq * (GLA_DK ** -0.5), gk, gv, gg, rq, rk, rv, rg, ga_f, ga_b, pad],
                             axis=1).astype(BF16)
    w_al = jnp.zeros((LANES, 2 * GLA_QK), F32)
    w_al = w_al.at[:GLA_LOWRANK, :GLA_QK].set(w_alpha_fwd)
    w_al = w_al.at[GLA_LOWRANK:2 * GLA_LOWRANK, GLA_QK:].set(w_alpha_bwd)
    b_al = jnp.concatenate([b_alpha_fwd, b_alpha_bwd])[None, :].astype(F32)
    return dict(
        anw=attn_norm_w[None, :].astype(F32), w_in=w_in_p, w_al=w_al.astype(BF16), b_al=b_al,
        gnw=gla_norm_w[None, :].astype(F32), rnw=ret_norm_w[None, :].astype(F32),
        rnb=ret_norm_b[None, :].astype(F32), w_out=w_out.astype(BF16),
        mnw=mlp_norm_w[None, :].astype(F32), w1=w_ff1.astype(BF16), w2=w_ff2.astype(BF16))


def _layer(x, lw, fnw, final_norm, interpret=False):
    batch, seq_len, _ = x.shape
    assert seq_len % STEP == 0 and seq_len % PROJ_STEP == 0
    ns = seq_len // STEP
    nsp = seq_len // PROJ_STEP
    nb = seq_len // BLOCK
    dmat, df, db, ef, eb, block_decay = _retention_tables()
    rot_base, rot_step = _rotary_tables(seq_len)

    def rev(b, n):
        return (b, nsp - 1 - n, 0)

    proj, cg, us = pl.pallas_call(
        functools.partial(_proj_kernel, ret_block_decay=block_decay),
        grid=(batch, nsp),
        in_specs=[
            pl.BlockSpec((None, PROJ_STEP, D_MODEL), rev),
            _const_spec((nsp, 2, LANES)),
            _const_spec((2, PROJ_STEP, LANES)),
            _const_spec((1, D_MODEL)),
            _const_spec((D_MODEL, W_IN_COLS)),
            _const_spec((LANES, 2 * GLA_QK)),
            _const_spec((1, 2 * GLA_QK)),
            _const_spec((BLOCK, RET_QK)),
        ],
        out_specs=[
            pl.BlockSpec((None, PROJ_STEP, P_COLS), rev),
            pl.BlockSpec((None, PROJ_STEP, 2 * GLA_QK), rev),
            pl.BlockSpec((None, PROJ_SUBS, STATE_ROWS, LANES), lambda b, n: (b, nsp - 1 - n, 0, 0)),
        ],
        out_shape=[
            jax.ShapeDtypeStruct((batch, seq_len, P_COLS), BF16),
            jax.ShapeDtypeStruct((batch, seq_len, 2 * GLA_QK), F32),
            jax.ShapeDtypeStruct((batch, nb, STATE_ROWS, LANES), BF16),
        ],
        scratch_shapes=[pltpu.VMEM((GLA_QK, GLA_DV), F32),
                        pltpu.VMEM((RET_HEADS, RET_DK, RET_DV), F32)],
        compiler_params=pltpu.CompilerParams(dimension_semantics=("arbitrary", "arbitrary"),
                                             vmem_limit_bytes=VMEM_LIMIT_BYTES),
        name="proj_bwd_state",
        interpret=interpret,
    )(x, rot_base, rot_step, lw["anw"], lw["w_in"], lw["w_al"], lw["b_al"], eb)

    total = batch * ns

    def cur(g):
        s = jnp.minimum(g, total - 1)
        return s // ns, s % ns

    def prev(g):
        s = jnp.maximum(g - 1, 0)
        return s // ns, s % ns

    return pl.pallas_call(
        functools.partial(_mix_kernel, steps_per_seq=ns, ret_block_decay=block_decay,
                          final_norm=final_norm),
        grid=(total + 1,),
        in_specs=[
            pl.BlockSpec((None, STEP, D_MODEL), lambda g: (*prev(g), 0)),
            pl.BlockSpec((None, STEP, P_COLS), lambda g: (*cur(g), 0)),
            pl.BlockSpec((None, STEP, 2 * GLA_QK), lambda g: (*cur(g), 0)),
            pl.BlockSpec((None, SUBS, STATE_ROWS, LANES), lambda g: (*cur(g), 0, 0)),
            _const_spec((RET_HEADS, BLOCK, BLOCK)),
            _const_spec((BLOCK, RET_QK)),
            _const_spec((BLOCK, RET_QK)),
            _const_spec((BLOCK, RET_QK)),
            _const_spec((1, GLA_DV)),
            _const_spec((1, RET_V)),
            _const_spec((1, RET_V)),
            _const_spec((D_MODEL, D_MODEL)),
            _const_spec((1, D_MODEL)),
            _const_spec((D_MODEL, D_FF)),
            _const_spec((D_FF, D_MODEL)),
            _const_spec((1, D_MODEL)),
        ],
        out_specs=pl.BlockSpec((None, STEP, D_MODEL), lambda g: (*prev(g), 0)),
        out_shape=jax.ShapeDtypeStruct((batch, seq_len, D_MODEL), F32),
        scratch_shapes=[pltpu.VMEM((GLA_QK, GLA_DV), F32),
                        pltpu.VMEM((RET_HEADS, RET_DK, RET_DV), F32),
                        pltpu.VMEM((STEP, D_MODEL), BF16),
                        pltpu.VMEM((STEP, D_FF), BF16)],
        compiler_params=pltpu.CompilerParams(dimension_semantics=("arbitrary",),
                                             vmem_limit_bytes=VMEM_LIMIT_BYTES),
        name="mix_mlp",
        interpret=interpret,
    )(x, proj, cg, us, dmat, df, db, ef, lw["gnw"], lw["rnw"], lw["rnb"], lw["w_out"],
      lw["mnw"], lw["w1"], lw["w2"], fnw)


def _trunk(x, layers, fnw, interpret=False):
    for i, lw in enumerate(layers):
        x = _layer(x, lw, fnw, final_norm=(i == len(layers) - 1), interpret=interpret)
    return x


def kernel(x_prompt, x_sample, attn_norm_w, w_in, w_alpha_fwd, b_alpha_fwd, w_alpha_bwd, b_alpha_bwd,
           gla_norm_w, ret_norm_w, ret_norm_b, w_out, mlp_norm_w, w_ff1, w_ff2, final_norm_w):
    stacked = (attn_norm_w, w_in, w_alpha_fwd, b_alpha_fwd, w_alpha_bwd, b_alpha_bwd,
               gla_norm_w, ret_norm_w, ret_norm_b, w_out, mlp_norm_w, w_ff1, w_ff2)
    layers = [_prepare_layer(*(w[l] for w in stacked)) for l in range(w_in.shape[0])]
    fnw = final_norm_w[None, :].astype(F32)
    return _trunk(x_prompt, layers, fnw), _trunk(x_sample, layers, fnw)
```

```python
import functools

import numpy as np
import jax
import jax.numpy as jnp
from jax import lax
from jax.experimental import pallas as pl
from jax.experimental.pallas import tpu as pltpu

F32 = jnp.float32
BF16 = jnp.bfloat16

D_MODEL = 1024
GLA_HEADS = 4
GLA_DK = 64
GLA_DV = 128
GLA_QK = GLA_HEADS * GLA_DK
GLA_V = GLA_HEADS * GLA_DV
GLA_LOWRANK = 16
GLA_NORMALIZER = 16.0
LOG_GATE_MIN = -1.0
RET_HEADS = 4
RET_DK = 128
RET_DV = 128
RET_QK = RET_HEADS * RET_DK
RET_V = RET_HEADS * RET_DV
D_FF = 4 * D_MODEL
ROPE_BASE = 10000.0
EPS = 1e-6

LANES = 128
BLOCK = 256
SUBS = 2
STEP = SUBS * BLOCK
PROJ_SUBS = 4
PROJ_STEP = PROJ_SUBS * BLOCK
CHUNK = 64
N_CHUNKS = BLOCK // CHUNK
FF_TILE = 512
VMEM_LIMIT_BYTES = 56 * 1024 * 1024

GQ, GK, GV, GG = 0, 256, 512, 1024
RQ, RK, RV, RG = 1536, 2048, 2560, 3072
P_COLS = 3584
GA = P_COLS
W_IN_COLS = P_COLS + LANES
STATE_ROWS = GLA_QK + RET_HEADS * RET_DK

NT = (((1,), (1,)), ((), ()))
TN = (((0,), (0,)), ((), ()))


def _dot(a, b, dims=None):
    if dims is None:
        return jnp.dot(a, b, preferred_element_type=F32)
    return lax.dot_general(a, b, dims, preferred_element_type=F32)


def _rms(x, w):
    return x * lax.rsqrt(jnp.mean(x * x, axis=-1, keepdims=True) + EPS) * w


def _silu(g):
    return g * jax.nn.sigmoid(g)


def _split3(a):
    h1 = a.astype(BF16)
    r1 = a - h1.astype(F32)
    h2 = r1.astype(BF16)
    r2 = r1 - h2.astype(F32)
    return h1, h2, r2.astype(BF16)


def _gla_state_increment(ke, v):
    blocks = []
    for pair in range(GLA_HEADS // 2):
        full = _dot(ke[:, pair * LANES:(pair + 1) * LANES],
                    v[:, pair * 2 * GLA_DV:(pair + 1) * 2 * GLA_DV], TN)
        blocks += [full[:GLA_DK, :GLA_DV], full[GLA_DK:, GLA_DV:]]
    return jnp.concatenate(blocks, axis=0)


def _as_column(row):
    return jnp.broadcast_to(row, (LANES, GLA_QK)).T


def _proj_kernel(x_ref, base_ref, rot_ref, anw_ref, win_ref, wal_ref, bal_ref, eb_ref,
                 proj_ref, cg_ref, us_ref, sg_ref, sr_ref, *, ret_block_decay):
    @pl.when(pl.program_id(1) == 0)
    def _():
        sg_ref[...] = jnp.zeros_like(sg_ref)
        sr_ref[...] = jnp.zeros_like(sr_ref)

    base = base_ref[pl.num_programs(1) - 1 - pl.program_id(1)]
    cos_b, sin_b = base[0:1, :], base[1:2, :]
    cos_i, sin_i = rot_ref[0], rot_ref[1]
    half_sign = jnp.where(lax.broadcasted_iota(jnp.int32, (1, LANES), 1) < RET_DK // 2, -1.0, 1.0)
    cos_all = cos_b * cos_i - sin_b * sin_i
    sin_all = (sin_b * cos_i + cos_b * sin_i) * half_sign

    row = lax.broadcasted_iota(jnp.int32, (CHUNK, 3 * CHUNK), 0)
    col = lax.broadcasted_iota(jnp.int32, (CHUNK, 3 * CHUNK), 1) % CHUNK
    tri_f = jnp.where(col <= row, 1.0, 0.0).astype(BF16)
    tri_b = jnp.where(col >= row, 1.0, 0.0).astype(BF16)

    def chunk_cumsums(tri, terms):
        return [_dot(tri, jnp.concatenate([t[a * CHUNK:(a + 1) * CHUNK] for t in terms], axis=0))
                for a in range(N_CHUNKS)]

    def block(sub):
        rows = slice(sub * BLOCK, (sub + 1) * BLOCK)
        h = _rms(x_ref[rows, :], anw_ref[...]).astype(BF16)

        def proj(lo, width):
            return _dot(h, win_ref[:, lo:lo + width])

        rg_ga = proj(RG, RET_V + LANES)
        proj_ref[rows, RG:RG + RET_V] = rg_ga[:, :RET_V].astype(BF16)
        ga = rg_ga[:, RET_V:].astype(BF16)
        z = _dot(ga, wal_ref[...]) + bal_ref[...]
        la = jnp.maximum(jax.nn.log_sigmoid(z) * (1.0 / GLA_NORMALIZER), LOG_GATE_MIN)
        la_f = _split3(la[:, :GLA_QK])
        la_b = _split3(la[:, GLA_QK:])

        gqk = proj(GQ, 2 * GLA_QK)
        proj_ref[rows, GQ:GQ + 2 * GLA_QK] = gqk.astype(BF16)
        gk = gqk[:, GLA_QK:]
        gv = proj(GV, GLA_V).astype(BF16)
        proj_ref[rows, GV:GV + GLA_V] = gv
        proj_ref[rows, GG:GG + GLA_V] = proj(GG, GLA_V).astype(BF16)
        rq_all = proj(RQ, RET_QK)
        rk_all = proj(RK, RET_QK)
        rv_all = proj(RV, RET_V).astype(BF16)
        proj_ref[rows, RV:RV + RET_V] = rv_all

        cum_b = chunk_cumsums(tri_b, la_b)
        cg_ref[rows, :GLA_QK] = jnp.concatenate(chunk_cumsums(tri_f, la_f), axis=0)
        cg_ref[rows, GLA_QK:] = jnp.concatenate(cum_b, axis=0)
        offset = jnp.zeros((1, GLA_QK), F32)
        for a in reversed(range(N_CHUNKS)):
            cum_b[a], offset = cum_b[a] + offset, offset + cum_b[a][0:1, :]
        cb = jnp.concatenate(cum_b, axis=0)

        cos = cos_all[rows, :]
        sin = sin_all[rows, :]
        for hh in range(RET_HEADS):
            lo = hh * RET_DK
            us_ref[sub, GLA_QK + lo:GLA_QK + lo + RET_DK, :] = sr_ref[hh].astype(BF16)
            q = rq_all[:, lo:lo + RET_DK]
            k = rk_all[:, lo:lo + RET_DK]
            q = q * cos + pltpu.roll(q, RET_DK // 2, axis=1) * sin
            k = k * cos + pltpu.roll(k, RET_DK // 2, axis=1) * sin
            proj_ref[rows, RQ + lo:RQ + lo + RET_DK] = q.astype(BF16)
            proj_ref[rows, RK + lo:RK + lo + RET_DK] = k.astype(BF16)
            v = rv_all[:, lo:lo + RET_DV]
            kd = (k * eb_ref[:, lo:lo + RET_DK]).astype(BF16)
            sr_ref[hh] = sr_ref[hh] * ret_block_decay[hh] + _dot(kd, v, TN)

        us_ref[sub, :GLA_QK, :] = sg_ref[...].astype(BF16)
        ctot = cb[0:1, :]
        kb = (gk * jnp.exp(ctot - cb)).astype(BF16)
        sg_ref[...] = sg_ref[...] * _as_column(jnp.exp(ctot)) + _gla_state_increment(kb, gv)

    for sub in reversed(range(PROJ_SUBS)):
        block(sub)


def _mix_kernel(xprev_ref, proj_ref, cg_ref, us_ref, dmat_ref, df_ref, db_ref, ef_ref,
                gnw_ref, rnw_ref, rnb_ref, wout_ref, mnw_ref, w1_ref, w2_ref, fnw_ref,
                y_ref, fg_ref, fr_ref, mixed_ref, ff_ref, *, steps_per_seq, ret_block_decay,
                final_norm):
    g = pl.program_id(0)

    @pl.when(g % steps_per_seq == 0)
    def _():
        fg_ref[...] = jnp.zeros_like(fg_ref)
        fr_ref[...] = jnp.zeros_like(fr_ref)

    @pl.when(g == 0)
    def _():
        mixed_ref[...] = jnp.zeros_like(mixed_ref)

    dense = {}

    def dense_head():
        x1 = xprev_ref[...] + _dot(mixed_ref[...], wout_ref[...])
        dense["h2"] = _rms(x1, mnw_ref[...]).astype(BF16)
        dense["acc"] = x1

    def dense_up(f):
        ff = jnp.maximum(_dot(dense["h2"], w1_ref[:, f * FF_TILE:(f + 1) * FF_TILE]), 0.0)
        ff_ref[:, f * FF_TILE:(f + 1) * FF_TILE] = (ff * ff).astype(BF16)

    def dense_down(half):
        cols = slice(half * (D_FF // 2), (half + 1) * (D_FF // 2))
        dense["acc"] = dense["acc"] + _dot(ff_ref[:, cols], w2_ref[cols, :])

    def dense_finish():
        y_ref[...] = _rms(dense["acc"], fnw_ref[...]) if final_norm else dense["acc"]

    n_ff = D_FF // FF_TILE
    dense_head()

    def gla_operands(sub, a, forward):
        r0 = sub * BLOCK + a * CHUNK
        cum = cg_ref[r0:r0 + CHUNK, 0:GLA_QK] if forward else cg_ref[r0:r0 + CHUNK, GLA_QK:2 * GLA_QK]
        e = jnp.exp(cum)
        e_edge = e[CHUNK - 1:CHUNK, :] if forward else e[0:1, :]
        q = proj_ref[r0:r0 + CHUNK, GQ:GQ + GLA_QK].astype(F32)
        k = proj_ref[r0:r0 + CHUNK, GK:GK + GLA_QK].astype(F32)
        ki = k * jnp.exp(-cum)
        return dict(qd=(q * e).astype(BF16), ki=ki, ke=(ki * e_edge).astype(BF16),
                    e_edge=e_edge, v=proj_ref[r0:r0 + CHUNK, GV:GV + GLA_V])

    sweeps = {True: list(range(N_CHUNKS)), False: list(reversed(range(N_CHUNKS)))}
    ops = {(sub, fw, a): gla_operands(sub, a, fw)
           for sub in range(SUBS) for fw in (True, False) for a in sweeps[fw]}

    ret = {}
    for sub in range(SUBS):
        rows = slice(sub * BLOCK, (sub + 1) * BLOCK)
        for h in range(RET_HEADS):
            lo = h * RET_DK
            q = proj_ref[rows, RQ + lo:RQ + lo + RET_DK]
            k = proj_ref[rows, RK + lo:RK + lo + RET_DK]
            qf = q.astype(F32)
            ret[(sub, h)] = dict(
                q=q, k=k, v=proj_ref[rows, RV + lo:RV + lo + RET_DV],
                q_both=jnp.concatenate([(qf * df_ref[:, lo:lo + RET_DK]).astype(BF16),
                                        (qf * db_ref[:, lo:lo + RET_DK]).astype(BF16)], axis=1),
                kd=(k.astype(F32) * ef_ref[:, lo:lo + RET_DK]).astype(BF16))

    n_pairs = GLA_HEADS // 2
    lane = lax.broadcasted_iota(jnp.int32, (CHUNK, LANES), 1)
    even_lanes = lane < GLA_DK
    for key, op in ops.items():
        _, fw, a = key
        if fw or a > 0:
            op["inc"] = _gla_state_increment(op["ke"], op["v"])
        op["s"] = []
        for pair in range(n_pairs):
            ki = op["ki"][:, pair * LANES:(pair + 1) * LANES]
            keys = jnp.concatenate([jnp.where(even_lanes, 0.0, ki), jnp.where(even_lanes, ki, 0.0)],
                                   axis=0)
            op["s"].append(_dot(op["qd"][:, pair * LANES:(pair + 1) * LANES], keys.T.astype(BF16)))
    for r in ret.values():
        r["inc"] = _dot(r["kd"], r["v"], TN)
        r["s"] = _dot(r["q"], r["k"], NT)
    for h in range(RET_HEADS):
        state = fr_ref[h]
        for sub in range(SUBS):
            r = ret[(sub, h)]
            entering = us_ref[sub, GLA_QK + h * RET_DK:GLA_QK + (h + 1) * RET_DK, :]
            states = jnp.concatenate([state.astype(BF16), entering], axis=0)
            r["inter"] = _dot(r["q_both"], states)
            state = state * ret_block_decay[h] + r["inc"]
        fr_ref[h] = state

    for f in range(n_ff):
        dense_up(f)

    ri = lax.broadcasted_iota(jnp.int32, (CHUNK, LANES), 0)
    ci = lane % CHUNK
    state = fg_ref[...]
    for sub in range(SUBS):
        for fw in (True, False):
            if not fw:
                fwd_state, state = state, us_ref[sub, :GLA_QK, :].astype(F32)
            keep = (ci <= ri) if fw else (ci > ri)
            for a in sweeps[fw]:
                op = ops[(sub, fw, a)]
                op["state"] = state.astype(BF16)
                op["p"] = [jnp.where(keep, s, 0.0).astype(BF16) for s in op["s"]]
                if "inc" in op:
                    state = state * _as_column(op["e_edge"]) + op["inc"]
        state = fwd_state
    fg_ref[...] = state
    for (sub, h), r in ret.items():
        r["p"] = (r["s"] * dmat_ref[h]).astype(BF16)

    for op in ops.values():
        outs = []
        for h in range(GLA_HEADS):
            pair, odd = divmod(h, 2)
            qd = op["qd"][:, pair * LANES:(pair + 1) * LANES]
            p = op["p"][pair]
            v = op["v"][:, h * GLA_DV:(h + 1) * GLA_DV]
            s_h = op["state"][h * GLA_DK:(h + 1) * GLA_DK, :]
            if odd:
                lhs = jnp.where(even_lanes, p, qd)
                rhs = jnp.concatenate([v, s_h], axis=0)
            else:
                lhs = jnp.where(even_lanes, qd, p)
                rhs = jnp.concatenate([s_h, v], axis=0)
            outs.append(_dot(lhs, rhs))
        op["o"] = jnp.concatenate(outs, axis=1)
    for r in ret.values():
        r["o"] = _dot(r["p"], r["v"]) + r["inter"]

    for half in range(2):
        dense_down(half)

    gnw = gnw_ref[...]
    for sub in range(SUBS):
        rows = slice(sub * BLOCK, (sub + 1) * BLOCK)
        o_gla = jnp.concatenate([ops[(sub, True, a)]["o"] + ops[(sub, False, a)]["o"]
                                 for a in range(N_CHUNKS)], axis=0)
        mixed = []
        for h in range(GLA_HEADS):
            lo = h * GLA_DV
            oh = _rms(o_gla[:, lo:lo + GLA_DV], gnw)
            mixed.append(oh * _silu(proj_ref[rows, GG + lo:GG + lo + GLA_DV].astype(F32)))
        for h in range(RET_HEADS):
            lo = h * RET_DV
            o = ret[(sub, h)]["o"]
            mu = jnp.mean(o, axis=-1, keepdims=True)
            oc = o - mu
            var = jnp.mean(oc * oc, axis=-1, keepdims=True)
            on = oc * lax.rsqrt(var + EPS) * rnw_ref[:, lo:lo + RET_DV] + rnb_ref[:, lo:lo + RET_DV]
            mixed.append(on * _silu(proj_ref[rows, RG + lo:RG + lo + RET_DV].astype(F32)))
        mixed_ref[rows, :] = jnp.concatenate(mixed, axis=1).astype(BF16)
    dense_finish()


def _const_spec(shape):
    zeros = (0,) * len(shape)
    return pl.BlockSpec(shape, lambda *_: zeros, pipeline_mode=pl.Buffered(1))


def _retention_tables():
    log_gamma = np.log1p(-np.power(2.0, -5.0 - np.arange(RET_HEADS, dtype=np.float64)))
    i = np.arange(BLOCK, dtype=np.float64)
    scale = RET_DK ** -0.5
    dist = np.abs(i[:, None] - i[None, :])
    dmat = np.exp(log_gamma[:, None, None] * dist[None]) * scale

    def expand(t):
        return np.repeat(t, RET_DK, axis=1).astype(np.float32)

    df = expand(np.exp(np.outer(i + 1.0, log_gamma)) * scale)
    db = expand(np.exp(np.outer(BLOCK - i, log_gamma)) * scale)
    ef = expand(np.exp(np.outer(BLOCK - 1.0 - i, log_gamma)))
    eb = expand(np.exp(np.outer(i, log_gamma)))
    block_decay = tuple(float(g) for g in np.exp(BLOCK * log_gamma))
    return dmat.astype(np.float32), df, db, ef, eb, block_decay


def _rotary_tables(seq_len):
    half = RET_DK // 2
    inv_freq = np.power(ROPE_BASE, -np.arange(half, dtype=np.float64) / half)

    def cos_sin(pos):
        ang = pos[:, None] * inv_freq[None, :]
        return np.stack([np.tile(np.cos(ang), (1, 2)), np.tile(np.sin(ang), (1, 2))]).astype(np.float32)

    base = cos_sin(np.arange(0, seq_len, PROJ_STEP, dtype=np.float64))
    return (np.ascontiguousarray(base.transpose(1, 0, 2)),
            cos_sin(np.arange(PROJ_STEP, dtype=np.float64)))


def _prepare_layer(attn_norm_w, w_in, w_alpha_fwd, b_alpha_fwd, w_alpha_bwd, b_alpha_bwd,
                   gla_norm_w, ret_norm_w, ret_norm_b, w_out, mlp_norm_w, w_ff1, w_ff2):
    o = np.cumsum((0, GLA_QK, GLA_QK, GLA_V, GLA_V, GLA_LOWRANK, GLA_LOWRANK,
                   RET_QK, RET_QK, RET_V, RET_V))
    w_in = w_in.astype(BF16)
    gq, gk, gv, gg, ga_f, ga_b, rq, rk, rv, rg = (w_in[:, o[i]:o[i + 1]] for i in range(10))
    pad = jnp.zeros((D_MODEL, LANES - 2 * GLA_LOWRANK), BF16)
    w_in_p = jnp.concatenate([gq * (GLA_DK ** -0.5), gk, gv, gg, rq, rk, rv, rg, ga_f, ga_b, pad],
                             axis=1)
    w_al = jnp.zeros((LANES, 2 * GLA_QK), F32)
    w_al = w_al.at[:GLA_LOWRANK, :GLA_QK].set(w_alpha_fwd)
    w_al = w_al.at[GLA_LOWRANK:2 * GLA_LOWRANK, GLA_QK:].set(w_alpha_bwd)
    b_al = jnp.concatenate([b_alpha_fwd, b_alpha_bwd])[None, :].astype(F32)
    return dict(
        anw=attn_norm_w[None, :].astype(F32), w_in=w_in_p, w_al=w_al.astype(BF16), b_al=b_al,
        gnw=gla_norm_w[None, :].astype(F32), rnw=ret_norm_w[None, :].astype(F32),
        rnb=ret_norm_b[None, :].astype(F32), w_out=w_out.astype(BF16),
        mnw=mlp_norm_w[None, :].astype(F32), w1=w_ff1.astype(BF16), w2=w_ff2.astype(BF16))


def _layer(x, lw, fnw, final_norm):
    batch, seq_len, _ = x.shape
    assert seq_len % STEP == 0 and seq_len % PROJ_STEP == 0
    ns = seq_len // STEP
    nsp = seq_len // PROJ_STEP
    nb = seq_len // BLOCK
    dmat, df, db, ef, eb, block_decay = _retention_tables()
    rot_base, rot_step = _rotary_tables(seq_len)

    def rev(b, n):
        return (b, nsp - 1 - n, 0)

    proj, cg, us = pl.pallas_call(
        functools.partial(_proj_kernel, ret_block_decay=block_decay),
        grid=(batch, nsp),
        in_specs=[
            pl.BlockSpec((None, PROJ_STEP, D_MODEL), rev),
            _const_spec((nsp, 2, LANES)),
            _const_spec((2, PROJ_STEP, LANES)),
            _const_spec((1, D_MODEL)),
            _const_spec((D_MODEL, W_IN_COLS)),
            _const_spec((LANES, 2 * GLA_QK)),
            _const_spec((1, 2 * GLA_QK)),
            _const_spec((BLOCK, RET_QK)),
        ],
        out_specs=[
            pl.BlockSpec((None, PROJ_STEP, P_COLS), rev),
            pl.BlockSpec((None, PROJ_STEP, 2 * GLA_QK), rev),
            pl.BlockSpec((None, PROJ_SUBS, STATE_ROWS, LANES), lambda b, n: (b, nsp - 1 - n, 0, 0)),
        ],
        out_shape=[
            jax.ShapeDtypeStruct((batch, seq_len, P_COLS), BF16),
            jax.ShapeDtypeStruct((batch, seq_len, 2 * GLA_QK), F32),
            jax.ShapeDtypeStruct((batch, nb, STATE_ROWS, LANES), BF16),
        ],
        scratch_shapes=[pltpu.VMEM((GLA_QK, GLA_DV), F32),
                        pltpu.VMEM((RET_HEADS, RET_DK, RET_DV), F32)],
        compiler_params=pltpu.CompilerParams(dimension_semantics=("arbitrary", "arbitrary"),
                                             vmem_limit_bytes=VMEM_LIMIT_BYTES),
        name="proj_bwd_state",
    )(x, rot_base, rot_step, lw["anw"], lw["w_in"], lw["w_al"], lw["b_al"], eb)

    total = batch * ns

    def cur(g):
        s = jnp.minimum(g, total - 1)
        return s // ns, s % ns

    def prev(g):
        s = jnp.maximum(g - 1, 0)
        return s // ns, s % ns

    return pl.pallas_call(
        functools.partial(_mix_kernel, steps_per_seq=ns, ret_block_decay=block_decay,
                          final_norm=final_norm),
        grid=(total + 1,),
        in_specs=[
            pl.BlockSpec((None, STEP, D_MODEL), lambda g: (*prev(g), 0)),
            pl.BlockSpec((None, STEP, P_COLS), lambda g: (*cur(g), 0)),
            pl.BlockSpec((None, STEP, 2 * GLA_QK), lambda g: (*cur(g), 0)),
            pl.BlockSpec((None, SUBS, STATE_ROWS, LANES), lambda g: (*cur(g), 0, 0)),
            _const_spec((RET_HEADS, BLOCK, BLOCK)),
            _const_spec((BLOCK, RET_QK)),
            _const_spec((BLOCK, RET_QK)),
            _const_spec((BLOCK, RET_QK)),
            _const_spec((1, GLA_DV)),
            _const_spec((1, RET_V)),
            _const_spec((1, RET_V)),
            _const_spec((D_MODEL, D_MODEL)),
            _const_spec((1, D_MODEL)),
            _const_spec((D_MODEL, D_FF)),
            _const_spec((D_FF, D_MODEL)),
            _const_spec((1, D_MODEL)),
        ],
        out_specs=pl.BlockSpec((None, STEP, D_MODEL), lambda g: (*prev(g), 0)),
        out_shape=jax.ShapeDtypeStruct((batch, seq_len, D_MODEL), F32),
        scratch_shapes=[pltpu.VMEM((GLA_QK, GLA_DV), F32),
                        pltpu.VMEM((RET_HEADS, RET_DK, RET_DV), F32),
                        pltpu.VMEM((STEP, D_MODEL), BF16),
                        pltpu.VMEM((STEP, D_FF), BF16)],
        compiler_params=pltpu.CompilerParams(dimension_semantics=("arbitrary",),
                                             vmem_limit_bytes=VMEM_LIMIT_BYTES),
        name="mix_mlp",
    )(x, proj, cg, us, dmat, df, db, ef, lw["gnw"], lw["rnw"], lw["rnb"], lw["w_out"],
      lw["mnw"], lw["w1"], lw["w2"], fnw)


def _trunk(x, layers, fnw):
    for i, lw in enumerate(layers):
        x = _layer(x, lw, fnw, final_norm=(i == len(layers) - 1))
    return x


def kernel(x_prompt, x_sample, attn_norm_w, w_in, w_alpha_fwd, b_alpha_fwd, w_alpha_bwd, b_alpha_bwd,
           gla_norm_w, ret_norm_w, ret_norm_b, w_out, mlp_norm_w, w_ff1, w_ff2, final_norm_w):
    stacked = (attn_norm_w, w_in, w_alpha_fwd, b_alpha_fwd, w_alpha_bwd, b_alpha_bwd,
               gla_norm_w, ret_norm_w, ret_norm_b, w_out, mlp_norm_w, w_ff1, w_ff2)
    layers = [_prepare_layer(*(w[l] for w in stacked)) for l in range(w_in.shape[0])]
    fnw = final_norm_w[None, :].astype(F32)
    return _trunk(x_prompt, layers, fnw), _trunk(x_sample, layers, fnw)
```

```python
import functools

import numpy as np
import jax
import jax.numpy as jnp
from jax import lax
from jax.experimental import pallas as pl
from jax.experimental.pallas import tpu as pltpu

F32 = jnp.float32
BF16 = jnp.bfloat16

D_MODEL = 1024
GLA_HEADS = 4
GLA_DK = 64
GLA_DV = 128
GLA_QK = GLA_HEADS * GLA_DK
GLA_V = GLA_HEADS * GLA_DV
GLA_LOWRANK = 16
GLA_NORMALIZER = 16.0
LOG_GATE_MIN = -1.0
RET_HEADS = 4
RET_DK = 128
RET_DV = 128
RET_QK = RET_HEADS * RET_DK
RET_V = RET_HEADS * RET_DV
D_FF = 4 * D_MODEL
ROPE_BASE = 10000.0
EPS = 1e-6

LANES = 128
BLOCK = 256
SUBS = 2
STEP = SUBS * BLOCK
PROJ_SUBS = 4
PROJ_STEP = PROJ_SUBS * BLOCK
PROJ_GROUP = 2
GROUP_ROWS = PROJ_GROUP * BLOCK
CHUNK = 64
N_CHUNKS = BLOCK // CHUNK
FF_TILE = 512
VMEM_LIMIT_BYTES = 56 * 1024 * 1024

GQ, GK, GV, GG = 0, 256, 512, 1024
RQ, RK, RV, RG = 1536, 2048, 2560, 3072
P_COLS = 3584
GA = P_COLS
W_IN_COLS = P_COLS + LANES
STATE_ROWS = GLA_QK + RET_HEADS * RET_DK

NT = (((1,), (1,)), ((), ()))
TN = (((0,), (0,)), ((), ()))


def _dot(a, b, dims=None):
    if dims is None:
        return jnp.dot(a, b, preferred_element_type=F32)
    return lax.dot_general(a, b, dims, preferred_element_type=F32)


def _rms(x, w):
    return x * lax.rsqrt(jnp.mean(x * x, axis=-1, keepdims=True) + EPS) * w


def _silu(g):
    return g * jax.nn.sigmoid(g)


def _split3(a):
    h1 = a.astype(BF16)
    r1 = a - h1.astype(F32)
    h2 = r1.astype(BF16)
    r2 = r1 - h2.astype(F32)
    return h1, h2, r2.astype(BF16)


def _gla_state_increment(ke, v):
    blocks = []
    for pair in range(GLA_HEADS // 2):
        full = _dot(ke[:, pair * LANES:(pair + 1) * LANES],
                    v[:, pair * 2 * GLA_DV:(pair + 1) * 2 * GLA_DV], TN)
        blocks += [full[:GLA_DK, :GLA_DV], full[GLA_DK:, GLA_DV:]]
    return jnp.concatenate(blocks, axis=0)


def _as_column(row):
    return jnp.broadcast_to(row, (LANES, GLA_QK)).T


def _proj_kernel(x_ref, base_ref, rot_ref, anw_ref, win_ref, wal_ref, bal_ref, eb_ref,
                 proj_ref, cg_ref, us_ref, sg_ref, sr_ref, *, ret_block_decay):
    @pl.when(pl.program_id(1) == 0)
    def _():
        sg_ref[...] = jnp.zeros_like(sg_ref)
        sr_ref[...] = jnp.zeros_like(sr_ref)

    base = base_ref[pl.num_programs(1) - 1 - pl.program_id(1)]
    cos_b, sin_b = base[0:1, :], base[1:2, :]
    cos_i, sin_i = rot_ref[0], rot_ref[1]
    half_sign = jnp.where(lax.broadcasted_iota(jnp.int32, (1, LANES), 1) < RET_DK // 2, -1.0, 1.0)
    cos_all = cos_b * cos_i - sin_b * sin_i
    sin_all = (sin_b * cos_i + cos_b * sin_i) * half_sign

    row = lax.broadcasted_iota(jnp.int32, (CHUNK, 3 * CHUNK), 0)
    col = lax.broadcasted_iota(jnp.int32, (CHUNK, 3 * CHUNK), 1) % CHUNK
    tri_f = jnp.where(col <= row, 1.0, 0.0).astype(BF16)
    tri_b = jnp.where(col >= row, 1.0, 0.0).astype(BF16)

    def chunk_cumsums(tri, terms):
        return [_dot(tri, jnp.concatenate([t[a * CHUNK:(a + 1) * CHUNK] for t in terms], axis=0))
                for a in range(N_CHUNKS)]

    def project(group):
        rows = slice(group * GROUP_ROWS, (group + 1) * GROUP_ROWS)
        h = _rms(x_ref[rows, :], anw_ref[...]).astype(BF16)

        def proj(lo, width):
            return _dot(h, win_ref[:, lo:lo + width])

        rg_ga = proj(RG, RET_V + LANES)
        proj_ref[rows, RG:RG + RET_V] = rg_ga[:, :RET_V].astype(BF16)
        ga = rg_ga[:, RET_V:].astype(BF16)
        z = _dot(ga, wal_ref[...]) + bal_ref[...]
        la = jnp.maximum(jax.nn.log_sigmoid(z) * (1.0 / GLA_NORMALIZER), LOG_GATE_MIN)

        gqk = proj(GQ, 2 * GLA_QK)
        proj_ref[rows, GQ:GQ + 2 * GLA_QK] = gqk.astype(BF16)
        gk = gqk[:, GLA_QK:]
        gv = proj(GV, GLA_V).astype(BF16)
        proj_ref[rows, GV:GV + GLA_V] = gv
        proj_ref[rows, GG:GG + GLA_V] = proj(GG, GLA_V).astype(BF16)
        rq_all = proj(RQ, RET_QK)
        rk_all = proj(RK, RET_QK)
        rv_all = proj(RV, RET_V).astype(BF16)
        proj_ref[rows, RV:RV + RET_V] = rv_all
        return dict(la=la, gk=gk, gv=gv, rq=rq_all, rk=rk_all, rv=rv_all)

    def finish(sub, group_arrays):
        rows = slice(sub * BLOCK, (sub + 1) * BLOCK)
        local = slice((sub % PROJ_GROUP) * BLOCK, (sub % PROJ_GROUP + 1) * BLOCK)
        la, gk, gv, rq_all, rk_all, rv_all = (group_arrays[n][local]
                                              for n in ("la", "gk", "gv", "rq", "rk", "rv"))
        la_f = _split3(la[:, :GLA_QK])
        la_b = _split3(la[:, GLA_QK:])

        cum_b = chunk_cumsums(tri_b, la_b)
        cg_ref[rows, :GLA_QK] = jnp.concatenate(chunk_cumsums(tri_f, la_f), axis=0)
        cg_ref[rows, GLA_QK:] = jnp.concatenate(cum_b, axis=0)
        offset = jnp.zeros((1, GLA_QK), F32)
        for a in reversed(range(N_CHUNKS)):
            cum_b[a], offset = cum_b[a] + offset, offset + cum_b[a][0:1, :]
        cb = jnp.concatenate(cum_b, axis=0)

        cos = cos_all[rows, :]
        sin = sin_all[rows, :]
        for hh in range(RET_HEADS):
            lo = hh * RET_DK
            us_ref[sub, GLA_QK + lo:GLA_QK + lo + RET_DK, :] = sr_ref[hh].astype(BF16)
            q = rq_all[:, lo:lo + RET_DK]
            k = rk_all[:, lo:lo + RET_DK]
            q = q * cos + pltpu.roll(q, RET_DK // 2, axis=1) * sin
            k = k * cos + pltpu.roll(k, RET_DK // 2, axis=1) * sin
            proj_ref[rows, RQ + lo:RQ + lo + RET_DK] = q.astype(BF16)
            proj_ref[rows, RK + lo:RK + lo + RET_DK] = k.astype(BF16)
            v = rv_all[:, lo:lo + RET_DV]
            kd = (k * eb_ref[:, lo:lo + RET_DK]).astype(BF16)
            sr_ref[hh] = sr_ref[hh] * ret_block_decay[hh] + _dot(kd, v, TN)

        us_ref[sub, :GLA_QK, :] = sg_ref[...].astype(BF16)
        ctot = cb[0:1, :]
        kb = (gk * jnp.exp(ctot - cb)).astype(BF16)
        sg_ref[...] = sg_ref[...] * _as_column(jnp.exp(ctot)) + _gla_state_increment(kb, gv)

    for group in reversed(range(PROJ_SUBS // PROJ_GROUP)):
        group_arrays = project(group)
        for sub in reversed(range(group * PROJ_GROUP, (group + 1) * PROJ_GROUP)):
            finish(sub, group_arrays)


def _mix_kernel(xprev_ref, proj_ref, cg_ref, us_ref, dmat_ref, df_ref, db_ref, ef_ref,
                gnw_ref, rnw_ref, rnb_ref, wout_ref, mnw_ref, w1_ref, w2_ref, fnw_ref,
                y_ref, fg_ref, fr_ref, mixed_ref, ff_ref, *, steps_per_seq, ret_block_decay,
                final_norm):
    g = pl.program_id(0)

    @pl.when(g % steps_per_seq == 0)
    def _():
        fg_ref[...] = jnp.zeros_like(fg_ref)
        fr_ref[...] = jnp.zeros_like(fr_ref)

    @pl.when(g == 0)
    def _():
        mixed_ref[...] = jnp.zeros_like(mixed_ref)

    dense = {}

    def dense_head():
        x1 = xprev_ref[...] + _dot(mixed_ref[...], wout_ref[...])
        dense["h2"] = _rms(x1, mnw_ref[...]).astype(BF16)
        dense["acc"] = x1

    def dense_up(f):
        ff = jnp.maximum(_dot(dense["h2"], w1_ref[:, f * FF_TILE:(f + 1) * FF_TILE]), 0.0)
        ff_ref[:, f * FF_TILE:(f + 1) * FF_TILE] = (ff * ff).astype(BF16)

    def dense_down(half):
        cols = slice(half * (D_FF // 2), (half + 1) * (D_FF // 2))
        dense["acc"] = dense["acc"] + _dot(ff_ref[:, cols], w2_ref[cols, :])

    def dense_finish():
        y_ref[...] = _rms(dense["acc"], fnw_ref[...]) if final_norm else dense["acc"]

    n_ff = D_FF // FF_TILE
    dense_head()

    def gla_operands(sub, a, forward):
        r0 = sub * BLOCK + a * CHUNK
        cum = cg_ref[r0:r0 + CHUNK, 0:GLA_QK] if forward else cg_ref[r0:r0 + CHUNK, GLA_QK:2 * GLA_QK]
        e = jnp.exp(cum)
        e_edge = e[CHUNK - 1:CHUNK, :] if forward else e[0:1, :]
        q = proj_ref[r0:r0 + CHUNK, GQ:GQ + GLA_QK].astype(F32)
        k = proj_ref[r0:r0 + CHUNK, GK:GK + GLA_QK].astype(F32)
        ki = k * jnp.exp(-cum)
        return dict(qd=(q * e).astype(BF16), ki=ki, ke=(ki * e_edge).astype(BF16),
                    e_edge=e_edge, v=proj_ref[r0:r0 + CHUNK, GV:GV + GLA_V])

    sweeps = {True: list(range(N_CHUNKS)), False: list(reversed(range(N_CHUNKS)))}
    ops = {(sub, fw, a): gla_operands(sub, a, fw)
           for sub in range(SUBS) for fw in (True, False) for a in sweeps[fw]}

    ret = {}
    for sub in range(SUBS):
        rows = slice(sub * BLOCK, (sub + 1) * BLOCK)
        for h in range(RET_HEADS):
            lo = h * RET_DK
            q = proj_ref[rows, RQ + lo:RQ + lo + RET_DK]
            k = proj_ref[rows, RK + lo:RK + lo + RET_DK]
            qf = q.astype(F32)
            ret[(sub, h)] = dict(
                q=q, k=k, v=proj_ref[rows, RV + lo:RV + lo + RET_DV],
                q_both=jnp.concatenate([(qf * df_ref[:, lo:lo + RET_DK]).astype(BF16),
                                        (qf * db_ref[:, lo:lo + RET_DK]).astype(BF16)], axis=1),
                kd=(k.astype(F32) * ef_ref[:, lo:lo + RET_DK]).astype(BF16))

    n_pairs = GLA_HEADS // 2
    lane = lax.broadcasted_iota(jnp.int32, (CHUNK, LANES), 1)
    even_lanes = lane < GLA_DK
    for key, op in ops.items():
        _, fw, a = key
        if fw or a > 0:
            op["inc"] = _gla_state_increment(op["ke"], op["v"])
        op["s"] = []
        for pair in range(n_pairs):
            ki = op["ki"][:, pair * LANES:(pair + 1) * LANES]
            keys = jnp.concatenate([jnp.where(even_lanes, 0.0, ki), jnp.where(even_lanes, ki, 0.0)],
                                   axis=0)
            op["s"].append(_dot(op["qd"][:, pair * LANES:(pair + 1) * LANES], keys.T.astype(BF16)))
    for r in ret.values():
        r["inc"] = _dot(r["kd"], r["v"], TN)
        r["s"] = _dot(r["q"], r["k"], NT)
    for h in range(RET_HEADS):
        state = fr_ref[h]
        for sub in range(SUBS):
            r = ret[(sub, h)]
            entering = us_ref[sub, GLA_QK + h * RET_DK:GLA_QK + (h + 1) * RET_DK, :]
            states = jnp.concatenate([state.astype(BF16), entering], axis=0)
            r["inter"] = _dot(r["q_both"], states)
            state = state * ret_block_decay[h] + r["inc"]
        fr_ref[h] = state

    for f in range(n_ff):
        dense_up(f)

    ri = lax.broadcasted_iota(jnp.int32, (CHUNK, LANES), 0)
    ci = lane % CHUNK
    state = fg_ref[...]
    for sub in range(SUBS):
        for fw in (True, False):
            if not fw:
                fwd_state, state = state, us_ref[sub, :GLA_QK, :].astype(F32)
            keep = (ci <= ri) if fw else (ci > ri)
            for a in sweeps[fw]:
                op = ops[(sub, fw, a)]
                op["state"] = state.astype(BF16)
                op["p"] = [jnp.where(keep, s, 0.0).astype(BF16) for s in op["s"]]
                if "inc" in op:
                    state = state * _as_column(op["e_edge"]) + op["inc"]
        state = fwd_state
    fg_ref[...] = state
    for (sub, h), r in ret.items():
        r["p"] = (r["s"] * dmat_ref[h]).astype(BF16)

    for op in ops.values():
        outs = []
        for h in range(GLA_HEADS):
            pair, odd = divmod(h, 2)
            qd = op["qd"][:, pair * LANES:(pair + 1) * LANES]
            p = op["p"][pair]
            v = op["v"][:, h * GLA_DV:(h + 1) * GLA_DV]
            s_h = op["state"][h * GLA_DK:(h + 1) * GLA_DK, :]
            if odd:
                lhs = jnp.where(even_lanes, p, qd)
                rhs = jnp.concatenate([v, s_h], axis=0)
            else:
                lhs = jnp.where(even_lanes, qd, p)
                rhs = jnp.concatenate([s_h, v], axis=0)
            outs.append(_dot(lhs, rhs))
        op["o"] = jnp.concatenate(outs, axis=1)
    for r in ret.values():
        r["o"] = _dot(r["p"], r["v"]) + r["inter"]

    for half in range(2):
        dense_down(half)

    gnw = gnw_ref[...]
    for sub in range(SUBS):
        rows = slice(sub * BLOCK, (sub + 1) * BLOCK)
        o_gla = jnp.concatenate([ops[(sub, True, a)]["o"] + ops[(sub, False, a)]["o"]
                                 for a in range(N_CHUNKS)], axis=0)
        mixed = []
        for h in range(GLA_HEADS):
            lo = h * GLA_DV
            oh = _rms(o_gla[:, lo:lo + GLA_DV], gnw)
            mixed.append(oh * _silu(proj_ref[rows, GG + lo:GG + lo + GLA_DV].astype(F32)))
        for h in range(RET_HEADS):
            lo = h * RET_DV
            o = ret[(sub, h)]["o"]
            mu = jnp.mean(o, axis=-1, keepdims=True)
            oc = o - mu
            var = jnp.mean(oc * oc, axis=-1, keepdims=True)
            on = oc * lax.rsqrt(var + EPS) * rnw_ref[:, lo:lo + RET_DV] + rnb_ref[:, lo:lo + RET_DV]
            mixed.append(on * _silu(proj_ref[rows, RG + lo:RG + lo + RET_DV].astype(F32)))
        mixed_ref[rows, :] = jnp.concatenate(mixed, axis=1).astype(BF16)
    dense_finish()


def _const_spec(shape):
    zeros = (0,) * len(shape)
    return pl.BlockSpec(shape, lambda *_: zeros, pipeline_mode=pl.Buffered(1))


def _retention_tables():
    log_gamma = np.log1p(-np.power(2.0, -5.0 - np.arange(RET_HEADS, dtype=np.float64)))
    i = np.arange(BLOCK, dtype=np.float64)
    scale = RET_DK ** -0.5
    dist = np.abs(i[:, None] - i[None, :])
    dmat = np.exp(log_gamma[:, None, None] * dist[None]) * scale

    def expand(t):
        return np.repeat(t, RET_DK, axis=1).astype(np.float32)

    df = expand(np.exp(np.outer(i + 1.0, log_gamma)) * scale)
    db = expand(np.exp(np.outer(BLOCK - i, log_gamma)) * scale)
    ef = expand(np.exp(np.outer(BLOCK - 1.0 - i, log_gamma)))
    eb = expand(np.exp(np.outer(i, log_gamma)))
    block_decay = tuple(float(g) for g in np.exp(BLOCK * log_gamma))
    return dmat.astype(np.float32), df, db, ef, eb, block_decay


def _rotary_tables(seq_len):
    half = RET_DK // 2
    inv_freq = np.power(ROPE_BASE, -np.arange(half, dtype=np.float64) / half)

    def cos_sin(pos):
        ang = pos[:, None] * inv_freq[None, :]
        return np.stack([np.tile(np.cos(ang), (1, 2)), np.tile(np.sin(ang), (1, 2))]).astype(np.float32)

    base = cos_sin(np.arange(0, seq_len, PROJ_STEP, dtype=np.float64))
    return (np.ascontiguousarray(base.transpose(1, 0, 2)),
            cos_sin(np.arange(PROJ_STEP, dtype=np.float64)))


def _prepare_layer(attn_norm_w, w_in, w_alpha_fwd, b_alpha_fwd, w_alpha_bwd, b_alpha_bwd,
                   gla_norm_w, ret_norm_w, ret_norm_b, w_out, mlp_norm_w, w_ff1, w_ff2):
    o = np.cumsum((0, GLA_QK, GLA_QK, GLA_V, GLA_V, GLA_LOWRANK, GLA_LOWRANK,
                   RET_QK, RET_QK, RET_V, RET_V))
    w_in = w_in.astype(BF16)
    gq, gk, gv, gg, ga_f, ga_b, rq, rk, rv, rg = (w_in[:, o[i]:o[i + 1]] for i in range(10))
    pad = jnp.zeros((D_MODEL, LANES - 2 * GLA_LOWRANK), BF16)
    w_in_p = jnp.concatenate([gq * (GLA_DK ** -0.5), gk, gv, gg, rq, rk, rv, rg, ga_f, ga_b, pad],
                             axis=1)
    w_al = jnp.zeros((LANES, 2 * GLA_QK), F32)
    w_al = w_al.at[:GLA_LOWRANK, :GLA_QK].set(w_alpha_fwd)
    w_al = w_al.at[GLA_LOWRANK:2 * GLA_LOWRANK, GLA_QK:].set(w_alpha_bwd)
    b_al = jnp.concatenate([b_alpha_fwd, b_alpha_bwd])[None, :].astype(F32)
    return dict(
        anw=attn_norm_w[None, :].astype(F32), w_in=w_in_p, w_al=w_al.astype(BF16), b_al=b_al,
        gnw=gla_norm_w[None, :].astype(F32), rnw=ret_norm_w[None, :].astype(F32),
        rnb=ret_norm_b[None, :].astype(F32), w_out=w_out.astype(BF16),
        mnw=mlp_norm_w[None, :].astype(F32), w1=w_ff1.astype(BF16), w2=w_ff2.astype(BF16))


def _layer(x, lw, fnw, final_norm):
    batch, seq_len, _ = x.shape
    assert seq_len % STEP == 0 and seq_len % PROJ_STEP == 0
    ns = seq_len // STEP
    nsp = seq_len // PROJ_STEP
    nb = seq_len // BLOCK
    dmat, df, db, ef, eb, block_decay = _retention_tables()
    rot_base, rot_step = _rotary_tables(seq_len)

    def rev(b, n):
        return (b, nsp - 1 - n, 0)

    proj, cg, us = pl.pallas_call(
        functools.partial(_proj_kernel, ret_block_decay=block_decay),
        grid=(batch, nsp),
        in_specs=[
            pl.BlockSpec((None, PROJ_STEP, D_MODEL), rev),
            _const_spec((nsp, 2, LANES)),
            _const_spec((2, PROJ_STEP, LANES)),
            _const_spec((1, D_MODEL)),
            _const_spec((D_MODEL, W_IN_COLS)),
            _const_spec((LANES, 2 * GLA_QK)),
            _const_spec((1, 2 * GLA_QK)),
            _const_spec((BLOCK, RET_QK)),
        ],
        out_specs=[
            pl.BlockSpec((None, PROJ_STEP, P_COLS), rev),
            pl.BlockSpec((None, PROJ_STEP, 2 * GLA_QK), rev),
            pl.BlockSpec((None, PROJ_SUBS, STATE_ROWS, LANES), lambda b, n: (b, nsp - 1 - n, 0, 0)),
        ],
        out_shape=[
            jax.ShapeDtypeStruct((batch, seq_len, P_COLS), BF16),
            jax.ShapeDtypeStruct((batch, seq_len, 2 * GLA_QK), F32),
            jax.ShapeDtypeStruct((batch, nb, STATE_ROWS, LANES), BF16),
        ],
        scratch_shapes=[pltpu.VMEM((GLA_QK, GLA_DV), F32),
                        pltpu.VMEM((RET_HEADS, RET_DK, RET_DV), F32)],
        compiler_params=pltpu.CompilerParams(dimension_semantics=("arbitrary", "arbitrary"),
                                             vmem_limit_bytes=VMEM_LIMIT_BYTES),
        name="proj_bwd_state",
    )(x, rot_base, rot_step, lw["anw"], lw["w_in"], lw["w_al"], lw["b_al"], eb)

    total = batch * ns

    def cur(g):
        s = jnp.minimum(g, total - 1)
        return s // ns, s % ns

    def prev(g):
        s = jnp.maximum(g - 1, 0)
        return s // ns, s % ns

    return pl.pallas_call(
        functools.partial(_mix_kernel, steps_per_seq=ns, ret_block_decay=block_decay,
                          final_norm=final_norm),
        grid=(total + 1,),
        in_specs=[
            pl.BlockSpec((None, STEP, D_MODEL), lambda g: (*prev(g), 0)),
            pl.BlockSpec((None, STEP, P_COLS), lambda g: (*cur(g), 0)),
            pl.BlockSpec((None, STEP, 2 * GLA_QK), lambda g: (*cur(g), 0)),
            pl.BlockSpec((None, SUBS, STATE_ROWS, LANES), lambda g: (*cur(g), 0, 0)),
            _const_spec((RET_HEADS, BLOCK, BLOCK)),
            _const_spec((BLOCK, RET_QK)),
            _const_spec((BLOCK, RET_QK)),
            _const_spec((BLOCK, RET_QK)),
            _const_spec((1, GLA_DV)),
            _const_spec((1, RET_V)),
            _const_spec((1, RET_V)),
            _const_spec((D_MODEL, D_MODEL)),
            _const_spec((1, D_MODEL)),
            _const_spec((D_MODEL, D_FF)),
            _const_spec((D_FF, D_MODEL)),
            _const_spec((1, D_MODEL)),
        ],
        out_specs=pl.BlockSpec((None, STEP, D_MODEL), lambda g: (*prev(g), 0)),
        out_shape=jax.ShapeDtypeStruct((batch, seq_len, D_MODEL), F32),
        scratch_shapes=[pltpu.VMEM((GLA_QK, GLA_DV), F32),
                        pltpu.VMEM((RET_HEADS, RET_DK, RET_DV), F32),
                        pltpu.VMEM((STEP, D_MODEL), BF16),
                        pltpu.VMEM((STEP, D_FF), BF16)],
        compiler_params=pltpu.CompilerParams(dimension_semantics=("arbitrary",),
                                             vmem_limit_bytes=VMEM_LIMIT_BYTES),
        name="mix_mlp",
    )(x, proj, cg, us, dmat, df, db, ef, lw["gnw"], lw["rnw"], lw["rnb"], lw["w_out"],
      lw["mnw"], lw["w1"], lw["w2"], fnw)


def _trunk(x, layers, fnw):
    for i, lw in enumerate(layers):
        x = _layer(x, lw, fnw, final_norm=(i == len(layers) - 1))
    return x


def kernel(x_prompt, x_sample, attn_norm_w, w_in, w_alpha_fwd, b_alpha_fwd, w_alpha_bwd, b_alpha_bwd,
           gla_norm_w, ret_norm_w, ret_norm_b, w_out, mlp_norm_w, w_ff1, w_ff2, final_norm_w):
    stacked = (attn_norm_w, w_in, w_alpha_fwd, b_alpha_fwd, w_alpha_bwd, b_alpha_bwd,
               gla_norm_w, ret_norm_w, ret_norm_b, w_out, mlp_norm_w, w_ff1, w_ff2)
    layers = [_prepare_layer(*(w[l] for w in stacked)) for l in range(w_in.shape[0])]
    fnw = final_norm_w[None, :].astype(F32)
    return _trunk(x_prompt, layers, fnw), _trunk(x_sample, layers, fnw)
```

```python
import functools

import numpy as np
import jax
import jax.numpy as jnp
from jax import lax
from jax.experimental import pallas as pl
from jax.experimental.pallas import tpu as pltpu

F32 = jnp.float32
BF16 = jnp.bfloat16

D_MODEL = 1024
GLA_HEADS = 4
GLA_DK = 64
GLA_DV = 128
GLA_QK = GLA_HEADS * GLA_DK
GLA_V = GLA_HEADS * GLA_DV
GLA_LOWRANK = 16
GLA_NORMALIZER = 16.0
LOG_GATE_MIN = -1.0
RET_HEADS = 4
RET_DK = 128
RET_DV = 128
RET_QK = RET_HEADS * RET_DK
RET_V = RET_HEADS * RET_DV
D_FF = 4 * D_MODEL
ROPE_BASE = 10000.0
EPS = 1e-6

LANES = 128
BLOCK = 256
SUBS = 2
STEP = SUBS * BLOCK
PROJ_SUBS = 4
PROJ_STEP = PROJ_SUBS * BLOCK
PROJ_GROUP = 4
GROUP_ROWS = PROJ_GROUP * BLOCK
CHUNK = 64
N_CHUNKS = BLOCK // CHUNK
FF_TILE = 512
VMEM_LIMIT_BYTES = 56 * 1024 * 1024

GQ, GK, GV, GG = 0, 256, 512, 1024
RQ, RK, RV, RG = 1536, 2048, 2560, 3072
P_COLS = 3584
GA = P_COLS
W_IN_COLS = P_COLS + LANES
STATE_ROWS = GLA_QK + RET_HEADS * RET_DK

NT = (((1,), (1,)), ((), ()))
TN = (((0,), (0,)), ((), ()))


def _dot(a, b, dims=None):
    if dims is None:
        return jnp.dot(a, b, preferred_element_type=F32)
    return lax.dot_general(a, b, dims, preferred_element_type=F32)


def _rms(x, w):
    return x * lax.rsqrt(jnp.mean(x * x, axis=-1, keepdims=True) + EPS) * w


def _silu(g):
    return g * jax.nn.sigmoid(g)


def _split3(a):
    h1 = a.astype(BF16)
    r1 = a - h1.astype(F32)
    h2 = r1.astype(BF16)
    r2 = r1 - h2.astype(F32)
    return h1, h2, r2.astype(BF16)


def _gla_state_increment(ke, v):
    blocks = []
    for pair in range(GLA_HEADS // 2):
        full = _dot(ke[:, pair * LANES:(pair + 1) * LANES],
                    v[:, pair * 2 * GLA_DV:(pair + 1) * 2 * GLA_DV], TN)
        blocks += [full[:GLA_DK, :GLA_DV], full[GLA_DK:, GLA_DV:]]
    return jnp.concatenate(blocks, axis=0)


def _as_column(row):
    return jnp.broadcast_to(row, (LANES, GLA_QK)).T


def _proj_kernel(x_ref, base_ref, rot_ref, anw_ref, win_ref, wal_ref, bal_ref, eb_ref,
                 proj_ref, cg_ref, us_ref, sg_ref, sr_ref, *, ret_block_decay):
    @pl.when(pl.program_id(1) == 0)
    def _():
        sg_ref[...] = jnp.zeros_like(sg_ref)
        sr_ref[...] = jnp.zeros_like(sr_ref)

    base = base_ref[pl.num_programs(1) - 1 - pl.program_id(1)]
    cos_b, sin_b = base[0:1, :], base[1:2, :]
    cos_i, sin_i = rot_ref[0], rot_ref[1]
    half_sign = jnp.where(lax.broadcasted_iota(jnp.int32, (1, LANES), 1) < RET_DK // 2, -1.0, 1.0)
    cos_all = cos_b * cos_i - sin_b * sin_i
    sin_all = (sin_b * cos_i + cos_b * sin_i) * half_sign

    row = lax.broadcasted_iota(jnp.int32, (CHUNK, 3 * CHUNK), 0)
    col = lax.broadcasted_iota(jnp.int32, (CHUNK, 3 * CHUNK), 1) % CHUNK
    tri_f = jnp.where(col <= row, 1.0, 0.0).astype(BF16)
    tri_b = jnp.where(col >= row, 1.0, 0.0).astype(BF16)

    def chunk_cumsums(tri, terms):
        return [_dot(tri, jnp.concatenate([t[a * CHUNK:(a + 1) * CHUNK] for t in terms], axis=0))
                for a in range(N_CHUNKS)]

    def project(group):
        rows = slice(group * GROUP_ROWS, (group + 1) * GROUP_ROWS)
        h = _rms(x_ref[rows, :], anw_ref[...]).astype(BF16)

        def proj(lo, width):
            return _dot(h, win_ref[:, lo:lo + width])

        rg_ga = proj(RG, RET_V + LANES)
        proj_ref[rows, RG:RG + RET_V] = rg_ga[:, :RET_V].astype(BF16)
        ga = rg_ga[:, RET_V:].astype(BF16)
        z = _dot(ga, wal_ref[...]) + bal_ref[...]
        la = jnp.maximum(jax.nn.log_sigmoid(z) * (1.0 / GLA_NORMALIZER), LOG_GATE_MIN)

        gqk = proj(GQ, 2 * GLA_QK)
        proj_ref[rows, GQ:GQ + 2 * GLA_QK] = gqk.astype(BF16)
        gk = gqk[:, GLA_QK:]
        gv = proj(GV, GLA_V).astype(BF16)
        proj_ref[rows, GV:GV + GLA_V] = gv
        proj_ref[rows, GG:GG + GLA_V] = proj(GG, GLA_V).astype(BF16)
        rq_all = proj(RQ, RET_QK)
        rk_all = proj(RK, RET_QK)
        rv_all = proj(RV, RET_V).astype(BF16)
        proj_ref[rows, RV:RV + RET_V] = rv_all
        return dict(la=la, gk=gk, gv=gv, rq=rq_all, rk=rk_all, rv=rv_all)

    def finish(sub, group_arrays):
        rows = slice(sub * BLOCK, (sub + 1) * BLOCK)
        local = slice((sub % PROJ_GROUP) * BLOCK, (sub % PROJ_GROUP + 1) * BLOCK)
        la, gk, gv, rq_all, rk_all, rv_all = (group_arrays[n][local]
                                              for n in ("la", "gk", "gv", "rq", "rk", "rv"))
        la_f = _split3(la[:, :GLA_QK])
        la_b = _split3(la[:, GLA_QK:])

        cum_b = chunk_cumsums(tri_b, la_b)
        cg_ref[rows, :GLA_QK] = jnp.concatenate(chunk_cumsums(tri_f, la_f), axis=0)
        cg_ref[rows, GLA_QK:] = jnp.concatenate(cum_b, axis=0)
        offset = jnp.zeros((1, GLA_QK), F32)
        for a in reversed(range(N_CHUNKS)):
            cum_b[a], offset = cum_b[a] + offset, offset + cum_b[a][0:1, :]
        cb = jnp.concatenate(cum_b, axis=0)

        cos = cos_all[rows, :]
        sin = sin_all[rows, :]
        for hh in range(RET_HEADS):
            lo = hh * RET_DK
            us_ref[sub, GLA_QK + lo:GLA_QK + lo + RET_DK, :] = sr_ref[hh].astype(BF16)
            q = rq_all[:, lo:lo + RET_DK]
            k = rk_all[:, lo:lo + RET_DK]
            q = q * cos + pltpu.roll(q, RET_DK // 2, axis=1) * sin
            k = k * cos + pltpu.roll(k, RET_DK // 2, axis=1) * sin
            proj_ref[rows, RQ + lo:RQ + lo + RET_DK] = q.astype(BF16)
            proj_ref[rows, RK + lo:RK + lo + RET_DK] = k.astype(BF16)
            v = rv_all[:, lo:lo + RET_DV]
            kd = (k * eb_ref[:, lo:lo + RET_DK]).astype(BF16)
            sr_ref[hh] = sr_ref[hh] * ret_block_decay[hh] + _dot(kd, v, TN)

        us_ref[sub, :GLA_QK, :] = sg_ref[...].astype(BF16)
        ctot = cb[0:1, :]
        kb = (gk * jnp.exp(ctot - cb)).astype(BF16)
        sg_ref[...] = sg_ref[...] * _as_column(jnp.exp(ctot)) + _gla_state_increment(kb, gv)

    for group in reversed(range(PROJ_SUBS // PROJ_GROUP)):
        group_arrays = project(group)
        for sub in reversed(range(group * PROJ_GROUP, (group + 1) * PROJ_GROUP)):
            finish(sub, group_arrays)


def _mix_kernel(xprev_ref, proj_ref, cg_ref, us_ref, dmat_ref, df_ref, db_ref, ef_ref,
                gnw_ref, rnw_ref, rnb_ref, wout_ref, mnw_ref, w1_ref, w2_ref, fnw_ref,
                y_ref, fg_ref, fr_ref, mixed_ref, ff_ref, *, steps_per_seq, ret_block_decay,
                final_norm):
    g = pl.program_id(0)

    @pl.when(g % steps_per_seq == 0)
    def _():
        fg_ref[...] = jnp.zeros_like(fg_ref)
        fr_ref[...] = jnp.zeros_like(fr_ref)

    @pl.when(g == 0)
    def _():
        mixed_ref[...] = jnp.zeros_like(mixed_ref)

    dense = {}

    def dense_head():
        x1 = xprev_ref[...] + _dot(mixed_ref[...], wout_ref[...])
        dense["h2"] = _rms(x1, mnw_ref[...]).astype(BF16)
        dense["acc"] = x1

    def dense_up(f):
        ff = jnp.maximum(_dot(dense["h2"], w1_ref[:, f * FF_TILE:(f + 1) * FF_TILE]), 0.0)
        ff_ref[:, f * FF_TILE:(f + 1) * FF_TILE] = (ff * ff).astype(BF16)

    def dense_down(half):
        cols = slice(half * (D_FF // 2), (half + 1) * (D_FF // 2))
        dense["acc"] = dense["acc"] + _dot(ff_ref[:, cols], w2_ref[cols, :])

    def dense_finish():
        y_ref[...] = _rms(dense["acc"], fnw_ref[...]) if final_norm else dense["acc"]

    n_ff = D_FF // FF_TILE
    dense_head()

    def gla_operands(sub, a, forward):
        r0 = sub * BLOCK + a * CHUNK
        cum = cg_ref[r0:r0 + CHUNK, 0:GLA_QK] if forward else cg_ref[r0:r0 + CHUNK, GLA_QK:2 * GLA_QK]
        e = jnp.exp(cum)
        e_edge = e[CHUNK - 1:CHUNK, :] if forward else e[0:1, :]
        q = proj_ref[r0:r0 + CHUNK, GQ:GQ + GLA_QK].astype(F32)
        k = proj_ref[r0:r0 + CHUNK, GK:GK + GLA_QK].astype(F32)
        ki = k * jnp.exp(-cum)
        return dict(qd=(q * e).astype(BF16), ki=ki, ke=(ki * e_edge).astype(BF16),
                    e_edge=e_edge, v=proj_ref[r0:r0 + CHUNK, GV:GV + GLA_V])

    sweeps = {True: list(range(N_CHUNKS)), False: list(reversed(range(N_CHUNKS)))}
    ops = {(sub, fw, a): gla_operands(sub, a, fw)
           for sub in range(SUBS) for fw in (True, False) for a in sweeps[fw]}

    ret = {}
    for sub in range(SUBS):
        rows = slice(sub * BLOCK, (sub + 1) * BLOCK)
        for h in range(RET_HEADS):
            lo = h * RET_DK
            q = proj_ref[rows, RQ + lo:RQ + lo + RET_DK]
            k = proj_ref[rows, RK + lo:RK + lo + RET_DK]
            qf = q.astype(F32)
            ret[(sub, h)] = dict(
                q=q, k=k, v=proj_ref[rows, RV + lo:RV + lo + RET_DV],
                q_both=jnp.concatenate([(qf * df_ref[:, lo:lo + RET_DK]).astype(BF16),
                                        (qf * db_ref[:, lo:lo + RET_DK]).astype(BF16)], axis=1),
                kd=(k.astype(F32) * ef_ref[:, lo:lo + RET_DK]).astype(BF16))

    n_pairs = GLA_HEADS // 2
    lane = lax.broadcasted_iota(jnp.int32, (CHUNK, LANES), 1)
    even_lanes = lane < GLA_DK
    for key, op in ops.items():
        _, fw, a = key
        if fw or a > 0:
            op["inc"] = _gla_state_increment(op["ke"], op["v"])
        op["s"] = []
        for pair in range(n_pairs):
            ki = op["ki"][:, pair * LANES:(pair + 1) * LANES]
            keys = jnp.concatenate([jnp.where(even_lanes, 0.0, ki), jnp.where(even_lanes, ki, 0.0)],
                                   axis=0)
            op["s"].append(_dot(op["qd"][:, pair * LANES:(pair + 1) * LANES], keys.T.astype(BF16)))
    for r in ret.values():
        r["inc"] = _dot(r["kd"], r["v"], TN)
        r["s"] = _dot(r["q"], r["k"], NT)
    for h in range(RET_HEADS):
        state = fr_ref[h]
        for sub in range(SUBS):
            r = ret[(sub, h)]
            entering = us_ref[sub, GLA_QK + h * RET_DK:GLA_QK + (h + 1) * RET_DK, :]
            states = jnp.concatenate([state.astype(BF16), entering], axis=0)
            r["inter"] = _dot(r["q_both"], states)
            state = state * ret_block_decay[h] + r["inc"]
        fr_ref[h] = state

    for f in range(n_ff):
        dense_up(f)

    ri = lax.broadcasted_iota(jnp.int32, (CHUNK, LANES), 0)
    ci = lane % CHUNK
    state = fg_ref[...]
    for sub in range(SUBS):
        for fw in (True, False):
            if not fw:
                fwd_state, state = state, us_ref[sub, :GLA_QK, :].astype(F32)
            keep = (ci <= ri) if fw else (ci > ri)
            for a in sweeps[fw]:
                op = ops[(sub, fw, a)]
                op["state"] = state.astype(BF16)
                op["p"] = [jnp.where(keep, s, 0.0).astype(BF16) for s in op["s"]]
                if "inc" in op:
                    state = state * _as_column(op["e_edge"]) + op["inc"]
        state = fwd_state
    fg_ref[...] = state
    for (sub, h), r in ret.items():
        r["p"] = (r["s"] * dmat_ref[h]).astype(BF16)

    for op in ops.values():
        outs = []
        for h in range(GLA_HEADS):
            pair, odd = divmod(h, 2)
            qd = op["qd"][:, pair * LANES:(pair + 1) * LANES]
            p = op["p"][pair]
            v = op["v"][:, h * GLA_DV:(h + 1) * GLA_DV]
            s_h = op["state"][h * GLA_DK:(h + 1) * GLA_DK, :]
            if odd:
                lhs = jnp.where(even_lanes, p, qd)
                rhs = jnp.concatenate([v, s_h], axis=0)
            else:
                lhs = jnp.where(even_lanes, qd, p)
                rhs = jnp.concatenate([s_h, v], axis=0)
            outs.append(_dot(lhs, rhs))
        op["o"] = jnp.concatenate(outs, axis=1)
    for r in ret.values():
        r["o"] = _dot(r["p"], r["v"]) + r["inter"]

    for half in range(2):
        dense_down(half)

    gnw = gnw_ref[...]
    for sub in range(SUBS):
        rows = slice(sub * BLOCK, (sub + 1) * BLOCK)
        o_gla = jnp.concatenate([ops[(sub, True, a)]["o"] + ops[(sub, False, a)]["o"]
                                 for a in range(N_CHUNKS)], axis=0)
        mixed = []
        for h in range(GLA_HEADS):
            lo = h * GLA_DV
            oh = _rms(o_gla[:, lo:lo + GLA_DV], gnw)
            mixed.append(oh * _silu(proj_ref[rows, GG + lo:GG + lo + GLA_DV].astype(F32)))
        for h in range(RET_HEADS):
            lo = h * RET_DV
            o = ret[(sub, h)]["o"]
            mu = jnp.mean(o, axis=-1, keepdims=True)
            oc = o - mu
            var = jnp.mean(oc * oc, axis=-1, keepdims=True)
            on = oc * lax.rsqrt(var + EPS) * rnw_ref[:, lo:lo + RET_DV] + rnb_ref[:, lo:lo + RET_DV]
            mixed.append(on * _silu(proj_ref[rows, RG + lo:RG + lo + RET_DV].astype(F32)))
        mixed_ref[rows, :] = jnp.concatenate(mixed, axis=1).astype(BF16)
    dense_finish()


def _const_spec(shape):
    zeros = (0,) * len(shape)
    return pl.BlockSpec(shape, lambda *_: zeros, pipeline_mode=pl.Buffered(1))


def _retention_tables():
    log_gamma = np.log1p(-np.power(2.0, -5.0 - np.arange(RET_HEADS, dtype=np.float64)))
    i = np.arange(BLOCK, dtype=np.float64)
    scale = RET_DK ** -0.5
    dist = np.abs(i[:, None] - i[None, :])
    dmat = np.exp(log_gamma[:, None, None] * dist[None]) * scale

    def expand(t):
        return np.repeat(t, RET_DK, axis=1).astype(np.float32)

    df = expand(np.exp(np.outer(i + 1.0, log_gamma)) * scale)
    db = expand(np.exp(np.outer(BLOCK - i, log_gamma)) * scale)
    ef = expand(np.exp(np.outer(BLOCK - 1.0 - i, log_gamma)))
    eb = expand(np.exp(np.outer(i, log_gamma)))
    block_decay = tuple(float(g) for g in np.exp(BLOCK * log_gamma))
    return dmat.astype(np.float32), df, db, ef, eb, block_decay


def _rotary_tables(seq_len):
    half = RET_DK // 2
    inv_freq = np.power(ROPE_BASE, -np.arange(half, dtype=np.float64) / half)

    def cos_sin(pos):
        ang = pos[:, None] * inv_freq[None, :]
        return np.stack([np.tile(np.cos(ang), (1, 2)), np.tile(np.sin(ang), (1, 2))]).astype(np.float32)

    base = cos_sin(np.arange(0, seq_len, PROJ_STEP, dtype=np.float64))
    return (np.ascontiguousarray(base.transpose(1, 0, 2)),
            cos_sin(np.arange(PROJ_STEP, dtype=np.float64)))


def _prepare_layer(attn_norm_w, w_in, w_alpha_fwd, b_alpha_fwd, w_alpha_bwd, b_alpha_bwd,
                   gla_norm_w, ret_norm_w, ret_norm_b, w_out, mlp_norm_w, w_ff1, w_ff2):
    o = np.cumsum((0, GLA_QK, GLA_QK, GLA_V, GLA_V, GLA_LOWRANK, GLA_LOWRANK,
                   RET_QK, RET_QK, RET_V, RET_V))
    w_in = w_in.astype(BF16)
    gq, gk, gv, gg, ga_f, ga_b, rq, rk, rv, rg = (w_in[:, o[i]:o[i + 1]] for i in range(10))
    pad = jnp.zeros((D_MODEL, LANES - 2 * GLA_LOWRANK), BF16)
    w_in_p = jnp.concatenate([gq * (GLA_DK ** -0.5), gk, gv, gg, rq, rk, rv, rg, ga_f, ga_b, pad],
                             axis=1)
    w_al = jnp.zeros((LANES, 2 * GLA_QK), F32)
    w_al = w_al.at[:GLA_LOWRANK, :GLA_QK].set(w_alpha_fwd)
    w_al = w_al.at[GLA_LOWRANK:2 * GLA_LOWRANK, GLA_QK:].set(w_alpha_bwd)
    b_al = jnp.concatenate([b_alpha_fwd, b_alpha_bwd])[None, :].astype(F32)
    return dict(
        anw=attn_norm_w[None, :].astype(F32), w_in=w_in_p, w_al=w_al.astype(BF16), b_al=b_al,
        gnw=gla_norm_w[None, :].astype(F32), rnw=ret_norm_w[None, :].astype(F32),
        rnb=ret_norm_b[None, :].astype(F32), w_out=w_out.astype(BF16),
        mnw=mlp_norm_w[None, :].astype(F32), w1=w_ff1.astype(BF16), w2=w_ff2.astype(BF16))


def _layer(x, lw, fnw, final_norm):
    batch, seq_len, _ = x.shape
    assert seq_len % STEP == 0 and seq_len % PROJ_STEP == 0
    ns = seq_len // STEP
    nsp = seq_len // PROJ_STEP
    nb = seq_len // BLOCK
    dmat, df, db, ef, eb, block_decay = _retention_tables()
    rot_base, rot_step = _rotary_tables(seq_len)

    def rev(b, n):
        return (b, nsp - 1 - n, 0)

    proj, cg, us = pl.pallas_call(
        functools.partial(_proj_kernel, ret_block_decay=block_decay),
        grid=(batch, nsp),
        in_specs=[
            pl.BlockSpec((None, PROJ_STEP, D_MODEL), rev),
            _const_spec((nsp, 2, LANES)),
            _const_spec((2, PROJ_STEP, LANES)),
            _const_spec((1, D_MODEL)),
            _const_spec((D_MODEL, W_IN_COLS)),
            _const_spec((LANES, 2 * GLA_QK)),
            _const_spec((1, 2 * GLA_QK)),
            _const_spec((BLOCK, RET_QK)),
        ],
        out_specs=[
            pl.BlockSpec((None, PROJ_STEP, P_COLS), rev),
            pl.BlockSpec((None, PROJ_STEP, 2 * GLA_QK), rev),
            pl.BlockSpec((None, PROJ_SUBS, STATE_ROWS, LANES), lambda b, n: (b, nsp - 1 - n, 0, 0)),
        ],
        out_shape=[
            jax.ShapeDtypeStruct((batch, seq_len, P_COLS), BF16),
            jax.ShapeDtypeStruct((batch, seq_len, 2 * GLA_QK), F32),
            jax.ShapeDtypeStruct((batch, nb, STATE_ROWS, LANES), BF16),
        ],
        scratch_shapes=[pltpu.VMEM((GLA_QK, GLA_DV), F32),
                        pltpu.VMEM((RET_HEADS, RET_DK, RET_DV), F32)],
        compiler_params=pltpu.CompilerParams(dimension_semantics=("arbitrary", "arbitrary"),
                                             vmem_limit_bytes=VMEM_LIMIT_BYTES),
        name="proj_bwd_state",
    )(x, rot_base, rot_step, lw["anw"], lw["w_in"], lw["w_al"], lw["b_al"], eb)

    total = batch * ns

    def cur(g):
        s = jnp.minimum(g, total - 1)
        return s // ns, s % ns

    def prev(g):
        s = jnp.maximum(g - 1, 0)
        return s // ns, s % ns

    return pl.pallas_call(
        functools.partial(_mix_kernel, steps_per_seq=ns, ret_block_decay=block_decay,
                          final_norm=final_norm),
        grid=(total + 1,),
        in_specs=[
            pl.BlockSpec((None, STEP, D_MODEL), lambda g: (*prev(g), 0)),
            pl.BlockSpec((None, STEP, P_COLS), lambda g: (*cur(g), 0)),
            pl.BlockSpec((None, STEP, 2 * GLA_QK), lambda g: (*cur(g), 0)),
            pl.BlockSpec((None, SUBS, STATE_ROWS, LANES), lambda g: (*cur(g), 0, 0)),
            _const_spec((RET_HEADS, BLOCK, BLOCK)),
            _const_spec((BLOCK, RET_QK)),
            _const_spec((BLOCK, RET_QK)),
            _const_spec((BLOCK, RET_QK)),
            _const_spec((1, GLA_DV)),
            _const_spec((1, RET_V)),
            _const_spec((1, RET_V)),
            _const_spec((D_MODEL, D_MODEL)),
            _const_spec((1, D_MODEL)),
            _const_spec((D_MODEL, D_FF)),
            _const_spec((D_FF, D_MODEL)),
            _const_spec((1, D_MODEL)),
        ],
        out_specs=pl.BlockSpec((None, STEP, D_MODEL), lambda g: (*prev(g), 0)),
        out_shape=jax.ShapeDtypeStruct((batch, seq_len, D_MODEL), F32),
        scratch_shapes=[pltpu.VMEM((GLA_QK, GLA_DV), F32),
                        pltpu.VMEM((RET_HEADS, RET_DK, RET_DV), F32),
                        pltpu.VMEM((STEP, D_MODEL), BF16),
                        pltpu.VMEM((STEP, D_FF), BF16)],
        compiler_params=pltpu.CompilerParams(dimension_semantics=("arbitrary",),
                                             vmem_limit_bytes=VMEM_LIMIT_BYTES),
        name="mix_mlp",
    )(x, proj, cg, us, dmat, df, db, ef, lw["gnw"], lw["rnw"], lw["rnb"], lw["w_out"],
      lw["mnw"], lw["w1"], lw["w2"], fnw)


def _trunk(x, layers, fnw):
    for i, lw in enumerate(layers):
        x = _layer(x, lw, fnw, final_norm=(i == len(layers) - 1))
    return x


def kernel(x_prompt, x_sample, attn_norm_w, w_in, w_alpha_fwd, b_alpha_fwd, w_alpha_bwd, b_alpha_bwd,
           gla_norm_w, ret_norm_w, ret_norm_b, w_out, mlp_norm_w, w_ff1, w_ff2, final_norm_w):
    stacked = (attn_norm_w, w_in, w_alpha_fwd, b_alpha_fwd, w_alpha_bwd, b_alpha_bwd,
               gla_norm_w, ret_norm_w, ret_norm_b, w_out, mlp_norm_w, w_ff1, w_ff2)
    layers = [_prepare_layer(*(w[l] for w in stacked)) for l in range(w_in.shape[0])]
    fnw = final_norm_w[None, :].astype(F32)
    return _trunk(x_prompt, layers, fnw), _trunk(x_sample, layers, fnw)
```
